```python
import jax, jax.numpy as jnp
from jax import lax
import numpy as np

D_MODEL = 2048
BATCH = 8
SEQ = 2048
DEPTH = 2

N_MIXERS = 2
NORM_EPS = 1e-6
GATED_NORM_EPS = 1e-5

SSM_EXPAND = 2
D_INNER = SSM_EXPAND * D_MODEL
SSM_HEAD_DIM = 64
SSM_HEADS = D_INNER // SSM_HEAD_DIM
SSM_GROUPS = 8
SSM_STATE = 128
SSM_CONV = 5
SSD_CHUNK = 128
D_XBC = D_INNER + 2 * SSM_GROUPS * SSM_STATE
D_SSM_IN = D_INNER + D_XBC + 2 * SSM_HEADS

GMLP_CHUNK = 128
D_GATE = 2 * D_MODEL
GMLP_GROUPS = 16
GMLP_GROUP_DIM = D_GATE // GMLP_GROUPS

N_EXPERTS = 32
TOP_K = 4
D_EXPERT = D_MODEL
SWIGLU_ALPHA = 1.702
SWIGLU_LIMIT = 7.0
MOE_BLOCK = 128

kernel_name = 'bidir_hybrid_ssd_gmlp_moe_adaln'

F32 = jnp.float32


def rmsnorm(x, g, eps=NORM_EPS):
    xf = x.astype(F32)
    y = xf * lax.rsqrt(jnp.mean(xf * xf, axis=-1, keepdims=True) + eps)
    return (y * g.astype(F32)).astype(x.dtype)


def modulate(h, shift, scale):
    return h * (1 + scale[:, None, :]) + shift[:, None, :]


def centred_depthwise_conv(x, w, b):
    k = w.shape[0]
    y = lax.conv_general_dilated(x, w[:, None, :].astype(x.dtype), window_strides=(1,),
                                 padding=[(k // 2, k // 2)],
                                 dimension_numbers=('NWC', 'WIO', 'NWC'),
                                 feature_group_count=x.shape[-1])
    return y + b.astype(x.dtype)


def ssd_chunked(x, dt, a, bm, cm):
    bsz, seq, n_heads, p = x.shape
    g, n = bm.shape[-2:]
    r = n_heads // g
    q = SSD_CHUNK
    nc = seq // q
    xc = x.astype(F32).reshape(bsz, nc, q, g, r, p)
    dtc = dt.reshape(bsz, nc, q, g, r)
    bc = bm.astype(F32).reshape(bsz, nc, q, g, n)
    cc = cm.astype(F32).reshape(bsz, nc, q, g, n)
    a_cum = jnp.cumsum(dtc * a.reshape(g, r), axis=2)
    a_cum_t = jnp.moveaxis(a_cum, 2, -1)
    dtx = xc * dtc[..., None]
    seg = a_cum_t[..., :, None] - a_cum_t[..., None, :]
    lower = jnp.tril(jnp.ones((q, q), bool))
    decay = jnp.exp(jnp.where(lower, seg, -jnp.inf))
    scores = jnp.einsum('bcign,bcjgn->bcgij', cc, bc)
    y_diag = jnp.einsum('bcgrij,bcjgrp->bcigrp', scores[:, :, :, None] * decay, dtx)
    decay_to_end = jnp.exp(a_cum_t[..., -1:] - a_cum_t)
    states = jnp.einsum('bcjgn,bcgrj,bcjgrp->bcgrpn', bc, decay_to_end, dtx)
    chunk_decay = jnp.exp(a_cum_t[..., -1])

    def step(h, inp):
        st, dec = inp
        return h * dec[..., None, None] + st, h

    h0 = jnp.zeros((bsz, g, r, p, n), F32)
    _, prev = lax.scan(step, h0, (jnp.moveaxis(states, 1, 0), jnp.moveaxis(chunk_decay, 1, 0)))
    prev = jnp.moveaxis(prev, 0, 1)
    y_off = jnp.einsum('bcign,bcgrpn->bcigrp', cc, prev) * jnp.exp(a_cum)[..., None]
    return (y_diag + y_off).reshape(bsz, seq, n_heads, p)


def gated_rmsnorm(y, z, g):
    yf = (y.astype(F32) * jax.nn.silu(z.astype(F32))).reshape(*y.shape[:-1], SSM_GROUPS, -1)
    yf = yf * lax.rsqrt(jnp.mean(yf * yf, axis=-1, keepdims=True) + GATED_NORM_EPS)
    return (yf.reshape(y.shape) * g.astype(F32)).astype(y.dtype)


def mamba2_bidir_mixer(h, w_in, conv_w, conv_b, dt_bias, a_log, d_skip, norm_g, w_out):
    bsz, seq, _ = h.shape
    zxbcdt = h @ w_in
    z = zxbcdt[..., :D_INNER]
    xbc = jax.nn.silu(centred_depthwise_conv(zxbcdt[..., D_INNER:D_INNER + D_XBC], conv_w, conv_b))
    dt_raw = zxbcdt[..., D_INNER + D_XBC:].reshape(bsz, seq, 2, SSM_HEADS)
    xs = xbc[..., :D_INNER].reshape(bsz, seq, SSM_HEADS, SSM_HEAD_DIM)
    bm = xbc[..., D_INNER:D_INNER + SSM_GROUPS * SSM_STATE].reshape(bsz, seq, SSM_GROUPS, SSM_STATE)
    cm = xbc[..., D_INNER + SSM_GROUPS * SSM_STATE:].reshape(bsz, seq, SSM_GROUPS, SSM_STATE)
    dt = jax.nn.softplus(dt_raw.astype(F32) + dt_bias.astype(F32))
    a = -jnp.exp(a_log.astype(F32))
    flip = lambda t: jnp.flip(t, axis=1)
    y_fwd = ssd_chunked(xs, dt[:, :, 0], a[0], bm, cm)
    y_bwd = flip(ssd_chunked(flip(xs), flip(dt[:, :, 1]), a[1], flip(bm), flip(cm)))
    y = y_fwd + y_bwd + d_skip.astype(F32)[:, None] * xs.astype(F32)
    y = y.reshape(bsz, seq, D_INNER).astype(h.dtype)
    return gated_rmsnorm(y, z, norm_g) @ w_out


def gmlp_chunk_mixer(h, w_in, b_in, ln_g, ln_b, w_s, b_s, w_out, b_out):
    bsz, seq, _ = h.shape
    uv = jax.nn.gelu(h @ w_in + b_in, approximate=False)
    u, v = uv[..., :D_GATE], uv[..., D_GATE:]
    vf = v.astype(F32)
    mu = jnp.mean(vf, axis=-1, keepdims=True)
    var = jnp.mean(jnp.square(vf - mu), axis=-1, keepdims=True)
    v = ((vf - mu) * lax.rsqrt(var + NORM_EPS) * ln_g.astype(F32) + ln_b.astype(F32)).astype(h.dtype)
    nc = seq // GMLP_CHUNK
    v = v.reshape(bsz, nc, GMLP_CHUNK, GMLP_GROUPS, GMLP_GROUP_DIM)
    v = jnp.einsum('gij,bcjgd->bcigd', w_s, v) + b_s.T[:, :, None]
    return (u * v.reshape(bsz, seq, D_GATE)) @ w_out + b_out


def clamped_swiglu(gu):
    gate, lin = gu[..., 0::2], gu[..., 1::2]
    gate = jnp.minimum(gate, SWIGLU_LIMIT)
    lin = jnp.clip(lin, -SWIGLU_LIMIT, SWIGLU_LIMIT)
    return gate * jax.nn.sigmoid(SWIGLU_ALPHA * gate) * (lin + 1)


def moe_ffn(h, w_router, b_router, w_up, b_up, w_down, b_down):
    t, d = h.shape
    logits = h.astype(F32) @ w_router.astype(F32) + b_router.astype(F32)
    top_logits, top_idx = lax.top_k(logits, TOP_K)
    gates = jax.nn.softmax(top_logits, axis=-1)
    n_assign = t * TOP_K
    flat_e = top_idx.reshape(-1)
    order = jnp.argsort(flat_e, stable=True)
    sorted_e = flat_e[order]
    counts = jnp.bincount(flat_e, length=N_EXPERTS)
    padded = (counts + MOE_BLOCK - 1) // MOE_BLOCK * MOE_BLOCK
    ends_pad = jnp.cumsum(padded)
    start_pad = ends_pad - padded
    start = jnp.cumsum(counts) - counts
    dest = start_pad[sorted_e] + (jnp.arange(n_assign) - start[sorted_e])
    n_rows = -(-n_assign // MOE_BLOCK) * MOE_BLOCK + N_EXPERTS * MOE_BLOCK
    n_blocks = n_rows // MOE_BLOCK
    row_token = jnp.full((n_rows,), t, jnp.int32).at[dest].set((order // TOP_K).astype(jnp.int32))
    row_gate = jnp.zeros((n_rows,), F32).at[dest].set(gates.reshape(-1)[order])
    block_expert = jnp.minimum(
        jnp.searchsorted(ends_pad, jnp.arange(n_blocks) * MOE_BLOCK, side='right'), N_EXPERTS - 1)
    h_pad = jnp.concatenate([h, jnp.zeros((1, d), h.dtype)], axis=0)
    xb = h_pad[row_token].reshape(n_blocks, MOE_BLOCK, d)

    def expert_block(args):
        xblk, e = args
        gu = xblk @ w_up[e] + b_up[e]
        return clamped_swiglu(gu) @ w_down[e] + b_down[e]

    yb = lax.map(expert_block, (xb, block_expert))
    y_rows = yb.reshape(n_rows, d) * row_gate[:, None].astype(h.dtype)
    out = jnp.zeros((t + 1, d), h.dtype).at[row_token].add(y_rows)
    return out[:t]


def setup_inputs(seed: int = 0) -> dict:
    key = jax.random.key(seed)
    ks = iter(jax.random.split(key, 40))
    n_a = (DEPTH + 1) // 2
    n_b = DEPTH // 2
    nrm = lambda shape, s: jax.random.normal(next(ks), shape, F32) * s
    dt0 = jnp.exp(jax.random.uniform(next(ks), (n_a, 2, SSM_HEADS), F32, np.log(1e-3), np.log(1e-1)))
    return {
        'x': nrm((BATCH, SEQ, D_MODEL), 1.0),
        'c': nrm((BATCH, D_MODEL), 1.0),
        'ada_w': nrm((DEPTH, D_MODEL, 6 * D_MODEL), 0.5 * D_MODEL ** -0.5),
        'ada_b': nrm((DEPTH, 6 * D_MODEL), 0.02),
        'norm_mix_g': 1.0 + nrm((DEPTH, D_MODEL), 0.05),
        'norm_ffn_g': 1.0 + nrm((DEPTH, D_MODEL), 0.05),
        'ssm_w_in': nrm((n_a, D_MODEL, D_SSM_IN), D_MODEL ** -0.5),
        'ssm_conv_w': nrm((n_a, SSM_CONV, D_XBC), SSM_CONV ** -0.5),
        'ssm_conv_b': nrm((n_a, D_XBC), 0.02),
        'ssm_dt_bias': dt0 + jnp.log(-jnp.expm1(-dt0)),
        'ssm_a_log': jnp.log(jax.random.uniform(next(ks), (n_a, 2, SSM_HEADS), F32, 1.0, 16.0)),
        'ssm_d': 1.0 + nrm((n_a, SSM_HEADS), 0.1),
        'ssm_norm_g': 1.0 + nrm((n_a, D_INNER), 0.05),
        'ssm_w_out': nrm((n_a, D_INNER, D_MODEL), D_INNER ** -0.5),
        'gmlp_w_in': nrm((n_b, D_MODEL, 2 * D_GATE), D_MODEL ** -0.5),
        'gmlp_b_in': nrm((n_b, 2 * D_GATE), 0.02),
        'gmlp_ln_g': 1.0 + nrm((n_b, D_GATE), 0.05),
        'gmlp_ln_b': nrm((n_b, D_GATE), 0.02),
        'gmlp_w_s': nrm((n_b, GMLP_GROUPS, GMLP_CHUNK, GMLP_CHUNK), GMLP_CHUNK ** -0.5),
        'gmlp_b_s': 1.0 + nrm((n_b, GMLP_GROUPS, GMLP_CHUNK), 0.1),
        'gmlp_w_out': nrm((n_b, D_GATE, D_MODEL), D_GATE ** -0.5),
        'gmlp_b_out': nrm((n_b, D_MODEL), 0.02),
        'moe_w_router': nrm((DEPTH, D_MODEL, N_EXPERTS), D_MODEL ** -0.5),
        'moe_b_router': nrm((DEPTH, N_EXPERTS), 0.01),
        'moe_w_up': nrm((DEPTH, N_EXPERTS, D_MODEL, 2 * D_EXPERT), D_MODEL ** -0.5),
        'moe_b_up': nrm((DEPTH, N_EXPERTS, 2 * D_EXPERT), 0.02),
        'moe_w_down': nrm((DEPTH, N_EXPERTS, D_EXPERT, D_MODEL), D_EXPERT ** -0.5),
        'moe_b_down': nrm((DEPTH, N_EXPERTS, D_MODEL), 0.02),
        'final_g': 1.0 + nrm((D_MODEL,), 0.05),
    }


def reference(x, c, ada_w, ada_b, norm_mix_g, norm_ffn_g,
              ssm_w_in, ssm_conv_w, ssm_conv_b, ssm_dt_bias, ssm_a_log, ssm_d, ssm_norm_g, ssm_w_out,
              gmlp_w_in, gmlp_b_in, gmlp_ln_g, gmlp_ln_b, gmlp_w_s, gmlp_b_s, gmlp_w_out, gmlp_b_out,
              moe_w_router, moe_b_router, moe_w_up, moe_b_up, moe_w_down, moe_b_down, final_g):
    bsz, seq, d = x.shape
    c_act = jax.nn.silu(c)
    for i in range(DEPTH):
        mod = c_act @ ada_w[i] + ada_b[i]
        sh1, sc1, g1, sh2, sc2, g2 = jnp.split(mod, 6, axis=-1)
        h = modulate(rmsnorm(x, norm_mix_g[i]), sh1, sc1)
        j = i // N_MIXERS
        if i % N_MIXERS == 0:
            y = mamba2_bidir_mixer(h, ssm_w_in[j], ssm_conv_w[j], ssm_conv_b[j], ssm_dt_bias[j],
                                   ssm_a_log[j], ssm_d[j], ssm_norm_g[j], ssm_w_out[j])
        else:
            y = gmlp_chunk_mixer(h, gmlp_w_in[j], gmlp_b_in[j], gmlp_ln_g[j], gmlp_ln_b[j],
                                 gmlp_w_s[j], gmlp_b_s[j], gmlp_w_out[j], gmlp_b_out[j])
        x = x + g1[:, None, :] * y
        h = modulate(rmsnorm(x, norm_ffn_g[i]), sh2, sc2)
        y = moe_ffn(h.reshape(bsz * seq, d), moe_w_router[i], moe_b_router[i], moe_w_up[i],
                    moe_b_up[i], moe_w_down[i], moe_b_down[i]).reshape(bsz, seq, d)
        x = x + g2[:, None, :] * y
    return rmsnorm(x, final_g)
```

```python
import functools

import jax
import jax.numpy as jnp
from jax import lax
from jax.experimental import pallas as pl
from jax.experimental.pallas import tpu as pltpu

F32 = jnp.float32
BF16 = jnp.bfloat16
HIGHEST = lax.Precision.HIGHEST

NORM_EPS = 1e-6
GATED_NORM_EPS = 1e-5
SSM_GROUPS = 8
SSD_CHUNK = 128
GMLP_CHUNK = 128
TOP_K = 4
SWIGLU_ALPHA = 1.702
SWIGLU_LIMIT = 7.0

VMEM_LIMIT_BYTES = 56 * 1024 * 1024
LANES = 128
SUBLANES = 8
MOE_ROWS = 512
MOE_F_TILES = 4


def _tile(n, pref, mult):
    if n <= pref:
        return n
    t = (pref // mult) * mult
    while t >= mult:
        if n % t == 0:
            return t
        t -= mult
    return n


def _params(sem):
    return pltpu.CompilerParams(dimension_semantics=sem, vmem_limit_bytes=VMEM_LIMIT_BYTES)


def _silu(x):
    return x * jax.nn.sigmoid(x)


def _gelu_erf(x):
    return 0.5 * x * (1.0 + lax.erf(x * (2.0 ** -0.5)))


def _rms_mod(x, g, sh, sc):
    ms = jnp.mean(x * x, axis=-1, keepdims=True)
    y = x * lax.rsqrt(ms + NORM_EPS) * g
    return y * (1.0 + sc) + sh


def _ada_kernel(c_ref, w_ref, b_ref, o_ref):
    ca = _silu(c_ref[...])
    o_ref[...] = jnp.dot(ca, w_ref[...], preferred_element_type=F32, precision=HIGHEST) + b_ref[...]


def _ada_mod(c, ada_w, ada_b):
    depth, d, n = ada_w.shape
    bsz = c.shape[0]
    tn = _tile(n, 1024, LANES)
    return pl.pallas_call(
        _ada_kernel,
        grid=(depth, n // tn),
        in_specs=[
            pl.BlockSpec((bsz, d), lambda l, j: (0, 0)),
            pl.BlockSpec((None, d, tn), lambda l, j: (l, 0, j)),
            pl.BlockSpec((None, 1, tn), lambda l, j: (l, 0, j)),
        ],
        out_specs=pl.BlockSpec((None, bsz, tn), lambda l, j: (l, 0, j)),
        out_shape=jax.ShapeDtypeStruct((depth, bsz, n), F32),
        compiler_params=_params(("arbitrary", "arbitrary")),
        name="ada_mod",
    )(c, ada_w, ada_b.reshape(depth, 1, n))


def _mm_kernel(*refs, prologue, epilogue):
    it = iter(refs)
    a_ref = next(it)
    if prologue == "norm":
        g_ref, sh_ref, sc_ref = next(it), next(it), next(it)
    w_ref = next(it)
    if epilogue in ("bias_gelu", "resid"):
        bias_ref = next(it)
    if epilogue == "resid":
        res_ref, gate_ref = next(it), next(it)
    o_ref = next(it)
    if prologue == "norm":
        h_ref = next(it)

        @pl.when(pl.program_id(2) == 0)
        def _():
            h = _rms_mod(a_ref[...], g_ref[...], sh_ref[...], sc_ref[...])
            h_ref[...] = h.astype(BF16)

        a = h_ref[...]
    else:
        a = a_ref[...]
    acc = jnp.dot(a, w_ref[...], preferred_element_type=F32)
    if epilogue == "store":
        o_ref[...] = acc.astype(o_ref.dtype)
    elif epilogue == "bias_gelu":
        o_ref[...] = _gelu_erf(acc + bias_ref[...]).astype(o_ref.dtype)
    elif epilogue == "resid":
        o_ref[...] = res_ref[...] + gate_ref[...] * (acc + bias_ref[...])


def _mm(a, w, *, prologue=None, norm=None, epilogue="store", bias=None, resid=None, gate=None,
        out_dtype=F32, tm_pref=1024, tn_pref=1024, name="mm"):
    bsz, seq, k = a.shape
    n = w.shape[1]
    tm = _tile(seq, tm_pref, SUBLANES * 2)
    tn = _tile(n, tn_pref, LANES)
    grid = (bsz, seq // tm, n // tn)
    in_specs = [pl.BlockSpec((None, tm, k), lambda b, i, j: (b, i, 0))]
    args = [a]
    scratch = []
    if prologue == "norm":
        g, sh, sc = norm
        in_specs += [
            pl.BlockSpec((1, k), lambda b, i, j: (0, 0)),
            pl.BlockSpec((None, 1, k), lambda b, i, j: (b, 0, 0)),
            pl.BlockSpec((None, 1, k), lambda b, i, j: (b, 0, 0)),
        ]
        args += [g.reshape(1, k), sh.reshape(bsz, 1, k), sc.reshape(bsz, 1, k)]
        scratch = [pltpu.VMEM((tm, k), BF16)]
    in_specs.append(pl.BlockSpec((k, tn), lambda b, i, j: (0, j)))
    args.append(w)
    if epilogue in ("bias_gelu", "resid"):
        in_specs.append(pl.BlockSpec((1, tn), lambda b, i, j: (0, j)))
        args.append(bias.reshape(1, n).astype(F32))
    if epilogue == "resid":
        in_specs += [
            pl.BlockSpec((None, tm, tn), lambda b, i, j: (b, i, j)),
            pl.BlockSpec((None, 1, tn), lambda b, i, j: (b, 0, j)),
        ]
        args += [resid, gate.reshape(bsz, 1, n)]
    return pl.pallas_call(
        functools.partial(_mm_kernel, prologue=prologue, epilogue=epilogue),
        grid=grid,
        in_specs=in_specs,
        out_specs=pl.BlockSpec((None, tm, tn), lambda b, i, j: (b, i, j)),
        out_shape=jax.ShapeDtypeStruct((bsz, seq, n), out_dtype),
        scratch_shapes=scratch,
        compiler_params=_params(("arbitrary", "arbitrary", "arbitrary")),
        name=name,
    )(*args)


def _conv_kernel(*refs, tl, halo):
    ins, outs, scr = refs[:15], refs[15:18], refs[18:21]
    i = pl.program_id(1)
    first = i == 0
    last = i == pl.num_programs(1) - 1
    for s in range(3):
        prev_ref, main_ref, next_ref, w_ref, b_ref = ins[5 * s:5 * s + 5]
        buf = scr[s]
        o_ref = outs[s]
        zeros = jnp.zeros((halo, buf.shape[1]), F32)
        buf[0:halo, :] = jnp.where(first, zeros, prev_ref[...])
        buf[halo:halo + tl, :] = main_ref[...]
        buf[halo + tl:2 * halo + tl, :] = jnp.where(last, zeros, next_ref[...])
        w = w_ref[...]
        kw = w.shape[0]
        acc = jnp.zeros((tl, buf.shape[1]), F32) + b_ref[...]
        for k in range(kw):
            acc = acc + buf[pl.ds(halo - kw // 2 + k, tl), :] * w[k:k + 1, :]
        o_ref[...] = _silu(acc).astype(o_ref.dtype)


def _conv_silu(zx, conv_w, conv_b, d_inner, n_state):
    bsz, seq, _ = zx.shape
    g = SSM_GROUPS
    rp = d_inner // g
    kw = conv_w.shape[0]
    halo = SUBLANES
    tl = _tile(seq, 512, SUBLANES)
    nh = tl // halo
    n_hblk = seq // halo
    widths = (rp, n_state, n_state)
    zoffs = (d_inner, 2 * d_inner, 2 * d_inner + g * n_state)
    coffs = (0, d_inner, d_inner + g * n_state)
    zx4 = zx.reshape(bsz, n_hblk, halo, zx.shape[-1])
    in_specs, args = [], []
    for wdt, zo, co in zip(widths, zoffs, coffs):
        zb, cb = zo // wdt, co // wdt
        in_specs += [
            pl.BlockSpec((None, None, halo, wdt),
                         lambda b, i, gg, zb=zb: (b, jnp.maximum(i * nh - 1, 0), 0, zb + gg)),
            pl.BlockSpec((None, tl, wdt), lambda b, i, gg, zb=zb: (b, i, zb + gg)),
            pl.BlockSpec((None, None, halo, wdt),
                         lambda b, i, gg, zb=zb: (b, jnp.minimum((i + 1) * nh, n_hblk - 1), 0, zb + gg)),
            pl.BlockSpec((kw, wdt), lambda b, i, gg, cb=cb: (0, cb + gg)),
            pl.BlockSpec((1, wdt), lambda b, i, gg, cb=cb: (0, cb + gg)),
        ]
        args += [zx4, zx, zx4, conv_w, conv_b.reshape(1, -1)]
    out_specs = [pl.BlockSpec((None, None, tl, wdt), lambda b, i, gg: (b, gg, i, 0)) for wdt in widths]
    out_shape = [jax.ShapeDtypeStruct((bsz, g, seq, wdt), BF16) for wdt in widths]
    scratch = [pltpu.VMEM((tl + 2 * halo, wdt), F32) for wdt in widths]
    return pl.pallas_call(
        functools.partial(_conv_kernel, tl=tl, halo=halo),
        grid=(bsz, seq // tl, g),
        in_specs=in_specs,
        out_specs=out_specs,
        out_shape=out_shape,
        scratch_shapes=scratch,
        compiler_params=_params(("arbitrary", "arbitrary", "arbitrary")),
        name="conv_silu",
    )(*args)


def _ssd_kernel(xs_ref, bm_ref, cm_ref, dt_ref, dtb_ref, alog_ref, y_ref,
                state_ref, cum_ref, cumt_ref, tott_ref, dtc_ref, totc_ref, *, reverse, n_heads, head_dim):
    q = xs_ref.shape[1]
    g_cnt = xs_ref.shape[0]
    r_cnt = n_heads // g_cnt
    off = n_heads if reverse else 0

    @pl.when(pl.program_id(1) == 0)
    def _():
        state_ref[...] = jnp.zeros_like(state_ref)

    dt_raw = dt_ref[...][:, off:off + n_heads]
    dt = jax.nn.softplus(dt_raw + dtb_ref[...][:, off:off + n_heads])
    a = dt * (-jnp.exp(alog_ref[...][:, off:off + n_heads]))
    ri = lax.broadcasted_iota(jnp.int32, (q, q), 0)
    ci = lax.broadcasted_iota(jnp.int32, (q, q), 1)
    keep = (ci >= ri) if reverse else (ci <= ri)
    tri = keep.astype(F32)
    cum = jnp.dot(tri, a, preferred_element_type=F32, precision=HIGHEST)
    ones = jnp.ones((q, q), F32)
    tot = jnp.dot(ones, a, preferred_element_type=F32, precision=HIGHEST)
    cum_t = cum.T
    tot_t = tot.T
    for g in range(g_cnt):
        sl = slice(g * r_cnt, (g + 1) * r_cnt)
        cum_ref[g, :, 0:r_cnt] = cum[:, sl]
        dtc_ref[g, :, 0:r_cnt] = dt[:, sl]
        totc_ref[g, :, 0:r_cnt] = tot[:, sl]
        cumt_ref[g, 0:r_cnt, :] = cum_t[sl, :]
        tott_ref[g, 0:r_cnt, :] = tot_t[sl, :]

    pair = (2 * head_dim == LANES)
    lane = lax.broadcasted_iota(jnp.int32, (q, LANES), 1)
    lo = lane < head_dim

    def expand(cols):
        parts = []
        for r in range(0, r_cnt, 2):
            parts.append(jnp.where(lo, cols[r], cols[r + 1]))
        return jnp.concatenate(parts, axis=1) if len(parts) > 1 else parts[0]

    def group_body(g, carry):
        xg = xs_ref[g].astype(F32)
        bg = bm_ref[g]
        cg = cm_ref[g]
        cum_g = cum_ref[g]
        dt_g = dtc_ref[g]
        cumt_g = cumt_ref[g]
        tott_g = tott_ref[g]
        scores = lax.dot_general(cg, bg, (((1,), (1,)), ((), ())), preferred_element_type=F32)
        cum_cols = [cum_g[:, r:r + 1] for r in range(r_cnt)]
        dt_cols = [dt_g[:, r:r + 1] for r in range(r_cnt)]
        tot_rows = [tott_g[r:r + 1, :] for r in range(r_cnt)]
        dtx = xg * expand(dt_cols)
        tot_g = totc_ref[g]
        wexit = [jnp.exp(tot_g[:, r:r + 1] - cum_cols[r]) for r in range(r_cnt)]
        xw = (dtx * expand(wexit)).astype(BF16)
        dtx_b = dtx.astype(BF16)
        ydiag_parts = []
        for p in range(0, r_cnt, 2):
            xp = dtx_b[:, p * head_dim:(p + 2) * head_dim]
            ys = []
            for r in (p, p + 1):
                seg = cum_cols[r] - cumt_g[r:r + 1, :]
                dec = jnp.where(keep, jnp.exp(seg), 0.0)
                m = (scores * dec).astype(BF16)
                ys.append(jnp.dot(m, xp, preferred_element_type=F32))
            ydiag_parts.append(jnp.where(lo, ys[0], ys[1]))
        y_diag = jnp.concatenate(ydiag_parts, axis=1) if len(ydiag_parts) > 1 else ydiag_parts[0]
        st = state_ref[g]
        y_off = jnp.dot(cg, st.astype(BF16), preferred_element_type=F32)
        ecum = expand([jnp.exp(c) for c in cum_cols])
        y_ref[g] = (y_diag + y_off * ecum).astype(y_ref.dtype)
        s_new = lax.dot_general(bg, xw, (((0,), (0,)), ((), ())), preferred_element_type=F32)
        cdec = expand([jnp.exp(t) for t in tot_rows])
        state_ref[g] = st * cdec[0:st.shape[0], :] + s_new
        return carry

    lax.fori_loop(0, g_cnt, group_body, 0)


def _ssd_sweep(xs, bm, cm, dt_raw, dt_bias, a_log, *, reverse):
    bsz, g, seq, rp = xs.shape
    n = bm.shape[-1]
    h2 = dt_raw.shape[-1]
    n_heads = h2 // 2
    head_dim = rp * g // n_heads
    q = SSD_CHUNK
    nc = seq // q
    assert 2 * head_dim == LANES and n == q and (n_heads // g) % 2 == 0
    cidx = (lambda c: nc - 1 - c) if reverse else (lambda c: c)
    return pl.pallas_call(
        functools.partial(_ssd_kernel, reverse=reverse, n_heads=n_heads, head_dim=head_dim),
        grid=(bsz, nc),
        in_specs=[
            pl.BlockSpec((None, g, q, rp), lambda b, c: (b, 0, cidx(c), 0)),
            pl.BlockSpec((None, g, q, n), lambda b, c: (b, 0, cidx(c), 0)),
            pl.BlockSpec((None, g, q, n), lambda b, c: (b, 0, cidx(c), 0)),
            pl.BlockSpec((None, q, h2), lambda b, c: (b, cidx(c), 0)),
            pl.BlockSpec((1, h2), lambda b, c: (0, 0)),
            pl.BlockSpec((1, h2), lambda b, c: (0, 0)),
        ],
        out_specs=pl.BlockSpec((None, g, q, rp), lambda b, c: (b, 0, cidx(c), 0)),
        out_shape=jax.ShapeDtypeStruct((bsz, g, seq, rp), BF16),
        scratch_shapes=[
            pltpu.VMEM((g, n, rp), F32),
            pltpu.VMEM((g, q, LANES), F32),
            pltpu.VMEM((g, SUBLANES, q), F32),
            pltpu.VMEM((g, SUBLANES, q), F32),
            pltpu.VMEM((g, q, LANES), F32),
            pltpu.VMEM((g, q, LANES), F32),
        ],
        compiler_params=_params(("arbitrary", "arbitrary")),
        name="ssd_bwd" if reverse else "ssd_fwd",
    )(xs, bm, cm, dt_raw, dt_bias.reshape(1, h2), a_log.reshape(1, h2))


def _ssm_out_kernel(yf_ref, yb_ref, xs_ref, z_ref, dsk_ref, ng_ref, w_ref, res_ref, gate_ref, o_ref, h_ref):
    @pl.when(pl.program_id(2) == 0)
    def _():
        g_cnt, _, rp = yf_ref.shape
        for g in range(g_cnt):
            cs = slice(g * rp, (g + 1) * rp)
            y = (yf_ref[g].astype(F32) + yb_ref[g].astype(F32)
                 + dsk_ref[:, cs] * xs_ref[g].astype(F32))
            yz = y * _silu(z_ref[:, cs])
            ms = jnp.mean(yz * yz, axis=-1, keepdims=True)
            h_ref[:, cs] = (yz * lax.rsqrt(ms + GATED_NORM_EPS) * ng_ref[:, cs]).astype(BF16)

    acc = jnp.dot(h_ref[...], w_ref[...], preferred_element_type=F32)
    o_ref[...] = res_ref[...] + gate_ref[...] * acc


def _ssm_out(yf, yb, xs, zx, d_row, norm_g, w_out, x, gate):
    bsz, g, seq, rp = yf.shape
    d_inner = g * rp
    n = w_out.shape[1]
    tm = _tile(seq, 256, SUBLANES * 2)
    tn = _tile(n, 1024, LANES)
    return pl.pallas_call(
        _ssm_out_kernel,
        grid=(bsz, seq // tm, n // tn),
        in_specs=[
            pl.BlockSpec((None, g, tm, rp), lambda b, i, j: (b, 0, i, 0)),
            pl.BlockSpec((None, g, tm, rp), lambda b, i, j: (b, 0, i, 0)),
            pl.BlockSpec((None, g, tm, rp), lambda b, i, j: (b, 0, i, 0)),
            pl.BlockSpec((None, tm, d_inner), lambda b, i, j: (b, i, 0)),
            pl.BlockSpec((1, d_inner), lambda b, i, j: (0, 0)),
            pl.BlockSpec((1, d_inner), lambda b, i, j: (0, 0)),
            pl.BlockSpec((d_inner, tn), lambda b, i, j: (0, j)),
            pl.BlockSpec((None, tm, tn), lambda b, i, j: (b, i, j)),
            pl.BlockSpec((None, 1, tn), lambda b, i, j: (b, 0, j)),
        ],
        out_specs=pl.BlockSpec((None, tm, tn), lambda b, i, j: (b, i, j)),
        out_shape=jax.ShapeDtypeStruct((bsz, seq, n), F32),
        scratch_shapes=[pltpu.VMEM((tm, d_inner), BF16)],
        compiler_params=_params(("arbitrary", "arbitrary", "arbitrary")),
        name="ssm_out",
    )(yf, yb, xs, zx, d_row, norm_g.reshape(1, d_inner), w_out, x, gate.reshape(bsz, 1, n))


def _gmlp_mix_kernel(u_ref, v_ref, lng_ref, lnb_ref, ws_ref, bst_ref, o_ref, *, chunk):
    rows, dg = v_ref.shape
    groups = ws_ref.shape[0]
    gd = dg // groups
    for c in range(rows // chunk):
        rs = slice(c * chunk, (c + 1) * chunk)
        v = v_ref[rs, :].astype(F32)
        mu = jnp.mean(v, axis=-1, keepdims=True)
        vc = v - mu
        var = jnp.mean(vc * vc, axis=-1, keepdims=True)
        vn = (vc * lax.rsqrt(var + NORM_EPS) * lng_ref[...] + lnb_ref[...]).astype(BF16)
        for g in range(groups):
            cs = slice(g * gd, (g + 1) * gd)
            vm = jnp.dot(ws_ref[g], vn[:, cs], preferred_element_type=F32) + bst_ref[:, g:g + 1]
            o_ref[rs, cs] = (u_ref[rs, cs].astype(F32) * vm).astype(o_ref.dtype)


def _gmlp_mix(uv, ln_g, ln_b, w_s, b_s):
    bsz, seq, two_dg = uv.shape
    dg = two_dg // 2
    groups = w_s.shape[0]
    tl = _tile(seq, 256, GMLP_CHUNK)
    return pl.pallas_call(
        functools.partial(_gmlp_mix_kernel, chunk=GMLP_CHUNK),
        grid=(bsz, seq // tl),
        in_specs=[
            pl.BlockSpec((None, tl, dg), lambda b, i: (b, i, 0)),
            pl.BlockSpec((None, tl, dg), lambda b, i: (b, i, 1)),
            pl.BlockSpec((1, dg), lambda b, i: (0, 0)),
            pl.BlockSpec((1, dg), lambda b, i: (0, 0)),
            pl.BlockSpec((groups, GMLP_CHUNK, GMLP_CHUNK), lambda b, i: (0, 0, 0)),
            pl.BlockSpec((GMLP_CHUNK, groups), lambda b, i: (0, 0)),
        ],
        out_specs=pl.BlockSpec((None, tl, dg), lambda b, i: (b, i, 0)),
        out_shape=jax.ShapeDtypeStruct((bsz, seq, dg), BF16),
        compiler_params=_params(("arbitrary", "arbitrary")),
        name="gmlp_mix",
    )(uv, uv, ln_g.reshape(1, dg), ln_b.reshape(1, dg), w_s.astype(BF16), b_s.T)


def _router_kernel(x_ref, g_ref, sh_ref, sc_ref, wrt_ref, br_ref, h_ref, idx_ref, gate_ref, rank_ref,
                   cnt_ref, carry_ref):
    step = pl.program_id(0)

    @pl.when(step == 0)
    def _():
        carry_ref[...] = jnp.zeros_like(carry_ref)

    h = _rms_mod(x_ref[...], g_ref[...], sh_ref[...], sc_ref[...])
    h_ref[...] = h
    n_exp = wrt_ref.shape[0]
    tm = h.shape[0]
    logits = lax.dot_general(wrt_ref[...], h, (((1,), (1,)), ((), ())),
                             preferred_element_type=F32, precision=HIGHEST) + br_ref[...]
    eidx = lax.broadcasted_iota(jnp.int32, (n_exp, tm), 0).astype(F32)
    work = logits
    tops, idxs, onehots = [], [], []
    for _ in range(TOP_K):
        m = jnp.max(work, axis=0, keepdims=True)
        sel = jnp.min(jnp.where(work == m, eidx, float(n_exp)), axis=0, keepdims=True)
        oh = eidx == sel
        tops.append(m)
        idxs.append(sel.astype(jnp.int32))
        onehots.append(oh)
        work = jnp.where(oh, -jnp.inf, work)
    exps = [jnp.exp(t - tops[0]) for t in tops]
    denom = exps[0] + exps[1] + exps[2] + exps[3]
    member = jnp.zeros((n_exp, tm), F32)
    for oh in onehots:
        member = member + oh.astype(F32)
    ti = lax.broadcasted_iota(jnp.int32, (tm, tm), 0)
    tj = lax.broadcasted_iota(jnp.int32, (tm, tm), 1)
    upper = (ti <= tj).astype(BF16)
    cinc = jnp.dot(member.astype(BF16), upper, preferred_element_type=F32)
    carry = carry_ref[...][:, 0:1]
    cexcl = cinc - member + carry
    ranks = [jnp.sum(jnp.where(oh, cexcl, 0.0), axis=0, keepdims=True) for oh in onehots]
    idx_ref[...] = jnp.concatenate(idxs, axis=0)
    gate_ref[...] = jnp.concatenate([e / denom for e in exps], axis=0)
    rank_ref[...] = jnp.concatenate(ranks, axis=0).astype(jnp.int32)
    new_carry = carry + jnp.sum(member, axis=1, keepdims=True)
    carry_ref[...] = jnp.broadcast_to(new_carry, carry_ref.shape)
    cnt_ref[...] = jnp.broadcast_to(new_carry, cnt_ref.shape).astype(jnp.int32)


def _router(x, norm_g, sh, sc, w_router, b_router):
    bsz, seq, d = x.shape
    t = bsz * seq
    n_exp = w_router.shape[1]
    tm = _tile(seq, 512, LANES)
    per_b = seq // tm
    return pl.pallas_call(
        _router_kernel,
        grid=(t // tm,),
        in_specs=[
            pl.BlockSpec((tm, d), lambda i: (i, 0)),
            pl.BlockSpec((1, d), lambda i: (0, 0)),
            pl.BlockSpec((None, 1, d), lambda i: (i // per_b, 0, 0)),
            pl.BlockSpec((None, 1, d), lambda i: (i // per_b, 0, 0)),
            pl.BlockSpec((n_exp, d), lambda i: (0, 0)),
            pl.BlockSpec((n_exp, 1), lambda i: (0, 0)),
        ],
        out_specs=[
            pl.BlockSpec((tm, d), lambda i: (i, 0)),
            pl.BlockSpec((TOP_K, tm), lambda i: (0, i)),
            pl.BlockSpec((TOP_K, tm), lambda i: (0, i)),
            pl.BlockSpec((TOP_K, tm), lambda i: (0, i)),
            pl.BlockSpec((n_exp, LANES), lambda i: (0, 0)),
        ],
        out_shape=[
            jax.ShapeDtypeStruct((t, d), F32),
            jax.ShapeDtypeStruct((TOP_K, t), jnp.int32),
            jax.ShapeDtypeStruct((TOP_K, t), F32),
            jax.ShapeDtypeStruct((TOP_K, t), jnp.int32),
            jax.ShapeDtypeStruct((n_exp, LANES), jnp.int32),
        ],
        scratch_shapes=[pltpu.VMEM((n_exp, LANES), F32)],
        compiler_params=_params(("arbitrary",)),
        name="moe_router",
    )(x.reshape(t, d), norm_g.reshape(1, d), sh.reshape(bsz, 1, d), sc.reshape(bsz, 1, d),
      w_router.T, b_router.reshape(n_exp, 1))


def _row_copy(h_hbm, xbuf, sem, tok, slot, r):
    return pltpu.make_async_copy(h_hbm.at[pl.ds(tok, 1)], xbuf.at[slot, pl.ds(r, 1)], sem.at[slot])


def _moe_kernel(ie_ref, cur_ref, nxt_ref, h_hbm, wg_ref, wl_ref, bg_ref, bl_ref, wd_ref, bd_ref, o_ref,
                xbuf, xb16, acc_ref, sem):
    it = pl.program_id(0)
    f = pl.program_id(1)
    n_items = pl.num_programs(0)
    n_f = pl.num_programs(1)
    rows = xbuf.shape[1]
    slot = lax.rem(it, 2)

    def issue(tok_ref, slot_):
        def body(r, c):
            _row_copy(h_hbm, xbuf, sem, tok_ref[0, r], slot_, r).start()
            return c
        lax.fori_loop(0, rows, body, 0)

    def wait_all(tok_ref, slot_):
        def body(r, c):
            _row_copy(h_hbm, xbuf, sem, tok_ref[0, r], slot_, r).wait()
            return c
        lax.fori_loop(0, rows, body, 0)

    @pl.when(f == 0)
    def _():
        @pl.when(it == 0)
        def _():
            issue(cur_ref, slot)

        wait_all(cur_ref, slot)

        @pl.when(it + 1 < n_items)
        def _():
            issue(nxt_ref, 1 - slot)

        xb16[...] = xbuf[slot].astype(BF16)

    x = xb16[...]
    gate = jnp.dot(x, wg_ref[...], preferred_element_type=F32) + bg_ref[...]
    lin = jnp.dot(x, wl_ref[...], preferred_element_type=F32) + bl_ref[...]
    gate = jnp.minimum(gate, SWIGLU_LIMIT)
    lin = jnp.clip(lin, -SWIGLU_LIMIT, SWIGLU_LIMIT)
    act = gate * jax.nn.sigmoid(SWIGLU_ALPHA * gate) * (lin + 1.0)
    part = jnp.dot(act.astype(BF16), wd_ref[...], preferred_element_type=F32)

    @pl.when(f == 0)
    def _():
        acc_ref[...] = part + bd_ref[...]

    @pl.when(f > 0)
    def _():
        acc_ref[...] += part

    @pl.when(f == n_f - 1)
    def _():
        o_ref[...] = acc_ref[...]


def _moe_experts(h, row_token, item_expert, wg, wl, bg, bl, wd, bd):
    t, d = h.shape
    n_exp, _, f_dim = wg.shape
    n_items = item_expert.shape[0]
    rows = MOE_ROWS
    n_f = MOE_F_TILES
    tf = f_dim // n_f
    rt = row_token.reshape(n_items, 1, rows)
    grid_spec = pltpu.PrefetchScalarGridSpec(
        num_scalar_prefetch=1,
        grid=(n_items, n_f),
        in_specs=[
            pl.BlockSpec((None, 1, rows), lambda i, f, ie: (i, 0, 0), memory_space=pltpu.SMEM),
            pl.BlockSpec((None, 1, rows), lambda i, f, ie: (jnp.minimum(i + 1, n_items - 1), 0, 0),
                         memory_space=pltpu.SMEM),
            pl.BlockSpec(memory_space=pl.ANY),
            pl.BlockSpec((None, d, tf), lambda i, f, ie: (ie[i], 0, f)),
            pl.BlockSpec((None, d, tf), lambda i, f, ie: (ie[i], 0, f)),
            pl.BlockSpec((None, 1, tf), lambda i, f, ie: (ie[i], 0, f)),
            pl.BlockSpec((None, 1, tf), lambda i, f, ie: (ie[i], 0, f)),
            pl.BlockSpec((None, tf, d), lambda i, f, ie: (ie[i], f, 0)),
            pl.BlockSpec((None, 1, d), lambda i, f, ie: (ie[i], 0, 0)),
        ],
        out_specs=pl.BlockSpec((rows, d), lambda i, f, ie: (i, 0)),
        scratch_shapes=[
            pltpu.VMEM((2, rows, d), F32),
            pltpu.VMEM((rows, d), BF16),
            pltpu.VMEM((rows, d), F32),
            pltpu.SemaphoreType.DMA((2,)),
        ],
    )
    return pl.pallas_call(
        _moe_kernel,
        grid_spec=grid_spec,
        out_shape=jax.ShapeDtypeStruct((n_items * rows, d), F32),
        compiler_params=_params(("arbitrary", "arbitrary")),
        name="moe_experts",
    )(item_expert, rt, rt, h, wg, wl, bg.reshape(n_exp, 1, f_dim), bl.reshape(n_exp, 1, f_dim), wd,
      bd.reshape(n_exp, 1, d))


def _combine_copy(y_hbm, buf, sem, row, slot, r):
    return pltpu.make_async_copy(y_hbm.at[pl.ds(row, 1)], buf.at[slot, pl.ds(r, 1)], sem.at[slot])


def _combine_kernel(cur_ref, nxt_ref, y_hbm, x_ref, gate_ref, g2_ref, fg_ref, o_ref, buf, sem, *, final_norm):
    it = pl.program_id(0)
    n_tiles = pl.num_programs(0)
    nrow = buf.shape[1]
    tq = x_ref.shape[0]
    slot = lax.rem(it, 2)

    def issue(ref, slot_):
        def body(r, c):
            _combine_copy(y_hbm, buf, sem, ref[0, r], slot_, r).start()
            return c
        lax.fori_loop(0, nrow, body, 0)

    def wait_all(ref, slot_):
        def body(r, c):
            _combine_copy(y_hbm, buf, sem, ref[0, r], slot_, r).wait()
            return c
        lax.fori_loop(0, nrow, body, 0)

    @pl.when(it == 0)
    def _():
        issue(cur_ref, slot)

    wait_all(cur_ref, slot)

    @pl.when(it + 1 < n_tiles)
    def _():
        issue(nxt_ref, 1 - slot)

    gates = gate_ref[...]
    y = jnp.zeros(x_ref.shape, F32)
    for k in range(TOP_K):
        y = y + gates[:, k:k + 1] * buf[slot, k * tq:(k + 1) * tq, :]
    out = x_ref[...] + g2_ref[...] * y
    if final_norm:
        ms = jnp.mean(out * out, axis=-1, keepdims=True)
        out = out * lax.rsqrt(ms + NORM_EPS) * fg_ref[...]
    o_ref[...] = out


def _moe_combine(y_rows, dest_km, gates_tk, x, g2, final_g, *, final_norm):
    bsz, seq, d = x.shape
    t = bsz * seq
    tq = _tile(seq, 128, SUBLANES)
    n_tiles = t // tq
    per_b = seq // tq
    dd = dest_km.reshape(n_tiles, 1, TOP_K * tq)
    grid_spec = pltpu.PrefetchScalarGridSpec(
        num_scalar_prefetch=0,
        grid=(n_tiles,),
        in_specs=[
            pl.BlockSpec((None, 1, TOP_K * tq), lambda i: (i, 0, 0), memory_space=pltpu.SMEM),
            pl.BlockSpec((None, 1, TOP_K * tq), lambda i: (jnp.minimum(i + 1, n_tiles - 1), 0, 0),
                         memory_space=pltpu.SMEM),
            pl.BlockSpec(memory_space=pl.ANY),
            pl.BlockSpec((tq, d), lambda i: (i, 0)),
            pl.BlockSpec((tq, TOP_K), lambda i: (i, 0)),
            pl.BlockSpec((None, 1, d), lambda i: (i // per_b, 0, 0)),
            pl.BlockSpec((1, d), lambda i: (0, 0)),
        ],
        out_specs=pl.BlockSpec((tq, d), lambda i: (i, 0)),
        scratch_shapes=[
            pltpu.VMEM((2, TOP_K * tq, d), F32),
            pltpu.SemaphoreType.DMA((2,)),
        ],
    )
    out = pl.pallas_call(
        functools.partial(_combine_kernel, final_norm=final_norm),
        grid_spec=grid_spec,
        out_shape=jax.ShapeDtypeStruct((t, d), F32),
        compiler_params=_params(("arbitrary",)),
        name="moe_combine",
    )(dd, dd, y_rows, x.reshape(t, d), gates_tk, g2.reshape(bsz, 1, d), final_g.reshape(1, d))
    return out.reshape(bsz, seq, d)


def _moe_ffn(x, norm_g, sh, sc, g2, w_router, b_router, w_up, b_up, w_down, b_down, final_g, *, final_norm):
    bsz, seq, d = x.shape
    t = bsz * seq
    n_exp = w_router.shape[1]
    rows = MOE_ROWS
    h, idx, gates, rank, cnt = _router(x, norm_g, sh, sc, w_router, b_router)
    counts = cnt[:, 0]
    padded = (counts + rows - 1) // rows * rows
    ends = jnp.cumsum(padded)
    starts = ends - padded
    dest = starts[idx] + rank
    n_items = (t * TOP_K) // rows + n_exp
    n_rows = n_items * rows
    tok = jnp.broadcast_to(jnp.arange(t, dtype=jnp.int32)[None, :], (TOP_K, t))
    row_token = jnp.zeros((n_rows,), jnp.int32).at[dest.reshape(-1)].set(tok.reshape(-1))
    item_expert = jnp.minimum(
        jnp.searchsorted(ends, jnp.arange(n_items, dtype=jnp.int32) * rows, side="right"),
        n_exp - 1).astype(jnp.int32)
    f_dim = w_up.shape[-1] // 2
    wg = w_up[:, :, 0::2].astype(BF16)
    wl = w_up[:, :, 1::2].astype(BF16)
    y_rows = _moe_experts(h, row_token, item_expert, wg, wl, b_up[:, 0::2], b_up[:, 1::2],
                          w_down.astype(BF16), b_down)
    tq = _tile(seq, 128, SUBLANES)
    dest_km = dest.reshape(TOP_K, t // tq, tq).transpose(1, 0, 2).reshape(-1)
    return _moe_combine(y_rows, dest_km, gates.T, x, g2, final_g, final_norm=final_norm)


def _mamba_mixer(x, norm_g, sh, sc, g1, w_in, conv_w, conv_b, dt_bias, a_log, d_skip, ssm_norm_g, w_out):
    d_inner = ssm_norm_g.shape[0]
    n_heads = d_skip.shape[0]
    head_dim = d_inner // n_heads
    d_xbc = conv_w.shape[1]
    n_state = (d_xbc - d_inner) // (2 * SSM_GROUPS)
    n_main = d_inner + d_xbc
    norm = (norm_g, sh, sc)
    zx = _mm(x, w_in[:, :n_main].astype(BF16), prologue="norm", norm=norm, name="ssm_in")
    dt_raw = _mm(x, w_in[:, n_main:].astype(BF16), prologue="norm", norm=norm, name="ssm_in_dt")
    xs, bm, cm = _conv_silu(zx, conv_w, conv_b, d_inner, n_state)
    yf = _ssd_sweep(xs, bm, cm, dt_raw, dt_bias, a_log, reverse=False)
    yb = _ssd_sweep(xs, bm, cm, dt_raw, dt_bias, a_log, reverse=True)
    d_row = jnp.repeat(d_skip.astype(F32), head_dim).reshape(1, d_inner)
    return _ssm_out(yf, yb, xs, zx, d_row, ssm_norm_g, w_out.astype(BF16), x, g1)


def _gmlp_mixer(x, norm_g, sh, sc, g1, w_in, b_in, ln_g, ln_b, w_s, b_s, w_out, b_out):
    uv = _mm(x, w_in.astype(BF16), prologue="norm", norm=(norm_g, sh, sc), epilogue="bias_gelu",
             bias=b_in, out_dtype=BF16, name="gmlp_in")
    gated = _gmlp_mix(uv, ln_g, ln_b, w_s, b_s)
    return _mm(gated, w_out.astype(BF16), epilogue="resid", bias=b_out, resid=x, gate=g1, name="gmlp_out")


def kernel(x, c, ada_w, ada_b, norm_mix_g, norm_ffn_g, ssm_w_in, ssm_conv_w, ssm_conv_b, ssm_dt_bias,
           ssm_a_log, ssm_d, ssm_norm_g, ssm_w_out, gmlp_w_in, gmlp_b_in, gmlp_ln_g, gmlp_ln_b, gmlp_w_s,
           gmlp_b_s, gmlp_w_out, gmlp_b_out, moe_w_router, moe_b_router, moe_w_up, moe_b_up, moe_w_down,
           moe_b_down, final_g):
    depth = ada_w.shape[0]
    d = x.shape[-1]
    mod = _ada_mod(c, ada_w, ada_b)
    for i in range(depth):
        sh1, sc1, g1, sh2, sc2, g2 = (mod[i, :, k * d:(k + 1) * d] for k in range(6))
        j = i // 2
        if i % 2 == 0:
            x = _mamba_mixer(x, norm_mix_g[i], sh1, sc1, g1, ssm_w_in[j], ssm_conv_w[j], ssm_conv_b[j],
                             ssm_dt_bias[j], ssm_a_log[j], ssm_d[j], ssm_norm_g[j], ssm_w_out[j])
        else:
            x = _gmlp_mixer(x, norm_mix_g[i], sh1, sc1, g1, gmlp_w_in[j], gmlp_b_in[j], gmlp_ln_g[j],
                            gmlp_ln_b[j], gmlp_w_s[j], gmlp_b_s[j], gmlp_w_out[j], gmlp_b_out[j])
        x = _moe_ffn(x, norm_ffn_g[i], sh2, sc2, g2, moe_w_router[i], moe_b_router[i], moe_w_up[i],
                     moe_b_up[i], moe_w_down[i], moe_b_down[i], final_g, final_norm=(i == depth - 1))
    return x
```

```python
import functools

import jax
import jax.numpy as jnp
from jax import lax
from jax.experimental import pallas as pl
from jax.experimental.pallas import tpu as pltpu

F32 = jnp.float32
BF16 = jnp.bfloat16
HIGHEST = lax.Precision.HIGHEST

NORM_EPS = 1e-6
GATED_NORM_EPS = 1e-5
SSM_GROUPS = 8
SSD_CHUNK = 128
GMLP_CHUNK = 128
TOP_K = 4
SWIGLU_ALPHA = 1.702
SWIGLU_LIMIT = 7.0

VMEM_LIMIT_BYTES = 56 * 1024 * 1024
LANES = 128
SUBLANES = 8
MOE_ROWS = 1024
MOE_ROW_BUCKETS = 4
MOE_TW = 256
COMBINE_TOKENS = 256


def _tile(n, pref, mult):
    if n <= pref:
        return n
    t = (pref // mult) * mult
    while t >= mult:
        if n % t == 0:
            return t
        t -= mult
    return n


def _params(sem):
    return pltpu.CompilerParams(dimension_semantics=sem, vmem_limit_bytes=VMEM_LIMIT_BYTES)


def _silu(x):
    return x * jax.nn.sigmoid(x)


def _gelu_erf(x):
    return 0.5 * x * (1.0 + lax.erf(x * (2.0 ** -0.5)))


def _pack_bf16_pairs(x):
    half = x.shape[1] // 2
    xb = x.astype(BF16).astype(F32)
    lo = lax.shift_right_logical(lax.bitcast_convert_type(xb[:, :half], jnp.int32), 16)
    hi = lax.bitcast_convert_type(xb[:, half:], jnp.int32) & jnp.int32(-65536)
    return lo | hi


def _unpack_bf16_pairs(p):
    lo = lax.bitcast_convert_type(lax.shift_left(p, 16), F32)
    hi = lax.bitcast_convert_type(p & jnp.int32(-65536), F32)
    return lo, hi


def _rms_mod(x, g, sh, sc):
    ms = jnp.mean(x * x, axis=-1, keepdims=True)
    y = x * lax.rsqrt(ms + NORM_EPS) * g
    return y * (1.0 + sc) + sh


def _ada_kernel(c_ref, w_ref, b_ref, o_ref):
    ca = _silu(c_ref[...])
    o_ref[...] = jnp.dot(ca, w_ref[...], preferred_element_type=F32, precision=HIGHEST) + b_ref[...]


def _ada_mod(c, ada_w, ada_b):
    depth, d, n = ada_w.shape
    bsz = c.shape[0]
    tn = _tile(n, 1024, LANES)
    return pl.pallas_call(
        _ada_kernel,
        grid=(depth, n // tn),
        in_specs=[
            pl.BlockSpec((bsz, d), lambda l, j: (0, 0)),
            pl.BlockSpec((None, d, tn), lambda l, j: (l, 0, j)),
            pl.BlockSpec((None, 1, tn), lambda l, j: (l, 0, j)),
        ],
        out_specs=pl.BlockSpec((None, bsz, tn), lambda l, j: (l, 0, j)),
        out_shape=jax.ShapeDtypeStruct((depth, bsz, n), F32),
        compiler_params=_params(("arbitrary", "arbitrary")),
        name="ada_mod",
    )(c, ada_w, ada_b.reshape(depth, 1, n))


def _mm_kernel(*refs, prologue, epilogue):
    it = iter(refs)
    a_ref = next(it)
    if prologue == "norm":
        g_ref, sh_ref, sc_ref = next(it), next(it), next(it)
    w_ref = next(it)
    if epilogue in ("bias_gelu", "resid"):
        bias_ref = next(it)
    if epilogue == "resid":
        res_ref, gate_ref = next(it), next(it)
    o_ref = next(it)
    if prologue == "norm":
        h_ref = next(it)

        @pl.when(pl.program_id(2) == 0)
        def _():
            h = _rms_mod(a_ref[...], g_ref[...], sh_ref[...], sc_ref[...])
            h_ref[...] = h.astype(BF16)

        a = h_ref[...]
    else:
        a = a_ref[...]
    acc = jnp.dot(a, w_ref[...], preferred_element_type=F32)
    if epilogue == "store":
        o_ref[...] = acc.astype(o_ref.dtype)
    elif epilogue == "bias_gelu":
        o_ref[...] = _gelu_erf(acc + bias_ref[...]).astype(o_ref.dtype)
    elif epilogue == "resid":
        o_ref[...] = res_ref[...] + gate_ref[...] * (acc + bias_ref[...])


def _mm(a, w, *, prologue=None, norm=None, epilogue="store", bias=None, resid=None, gate=None,
        out_dtype=F32, tm_pref=1024, tn_pref=1024, name="mm"):
    bsz, seq, k = a.shape
    n = w.shape[1]
    tm = _tile(seq, tm_pref, SUBLANES * 2)
    tn = _tile(n, tn_pref, LANES)
    grid = (bsz, seq // tm, n // tn)
    in_specs = [pl.BlockSpec((None, tm, k), lambda b, i, j: (b, i, 0))]
    args = [a]
    scratch = []
    if prologue == "norm":
        g, sh, sc = norm
        in_specs += [
            pl.BlockSpec((1, k), lambda b, i, j: (0, 0)),
            pl.BlockSpec((None, 1, k), lambda b, i, j: (b, 0, 0)),
            pl.BlockSpec((None, 1, k), lambda b, i, j: (b, 0, 0)),
        ]
        args += [g.reshape(1, k), sh.reshape(bsz, 1, k), sc.reshape(bsz, 1, k)]
        scratch = [pltpu.VMEM((tm, k), BF16)]
    in_specs.append(pl.BlockSpec((k, tn), lambda b, i, j: (0, j)))
    args.append(w)
    if epilogue in ("bias_gelu", "resid"):
        in_specs.append(pl.BlockSpec((1, tn), lambda b, i, j: (0, j)))
        args.append(bias.reshape(1, n).astype(F32))
    if epilogue == "resid":
        in_specs += [
            pl.BlockSpec((None, tm, tn), lambda b, i, j: (b, i, j)),
            pl.BlockSpec((None, 1, tn), lambda b, i, j: (b, 0, j)),
        ]
        args += [resid, gate.reshape(bsz, 1, n)]
    return pl.pallas_call(
        functools.partial(_mm_kernel, prologue=prologue, epilogue=epilogue),
        grid=grid,
        in_specs=in_specs,
        out_specs=pl.BlockSpec((None, tm, tn), lambda b, i, j: (b, i, j)),
        out_shape=jax.ShapeDtypeStruct((bsz, seq, n), out_dtype),
        scratch_shapes=scratch,
        compiler_params=_params(("arbitrary", "arbitrary", "arbitrary")),
        name=name,
    )(*args)


def _conv_kernel(*refs, tl, halo):
    ins, outs, scr = refs[:15], refs[15:18], refs[18:21]
    i = pl.program_id(1)
    first = i == 0
    last = i == pl.num_programs(1) - 1
    for s in range(3):
        prev_ref, main_ref, next_ref, w_ref, b_ref = ins[5 * s:5 * s + 5]
        buf = scr[s]
        o_ref = outs[s]
        zeros = jnp.zeros((halo, buf.shape[1]), F32)
        buf[0:halo, :] = jnp.where(first, zeros, prev_ref[...])
        buf[halo:halo + tl, :] = main_ref[...]
        buf[halo + tl:2 * halo + tl, :] = jnp.where(last, zeros, next_ref[...])
        w = w_ref[...]
        kw = w.shape[0]
        acc = jnp.zeros((tl, buf.shape[1]), F32) + b_ref[...]
        for k in range(kw):
            acc = acc + buf[pl.ds(halo - kw // 2 + k, tl), :] * w[k:k + 1, :]
        o_ref[...] = _silu(acc).astype(o_ref.dtype)


def _conv_silu(zx, conv_w, conv_b, d_inner, n_state):
    bsz, seq, _ = zx.shape
    g = SSM_GROUPS
    rp = d_inner // g
    kw = conv_w.shape[0]
    halo = SUBLANES
    tl = _tile(seq, 512, SUBLANES)
    nh = tl // halo
    n_hblk = seq // halo
    widths = (rp, n_state, n_state)
    zoffs = (d_inner, 2 * d_inner, 2 * d_inner + g * n_state)
    coffs = (0, d_inner, d_inner + g * n_state)
    zx4 = zx.reshape(bsz, n_hblk, halo, zx.shape[-1])
    in_specs, args = [], []
    for wdt, zo, co in zip(widths, zoffs, coffs):
        zb, cb = zo // wdt, co // wdt
        in_specs += [
            pl.BlockSpec((None, None, halo, wdt),
                         lambda b, i, gg, zb=zb: (b, jnp.maximum(i * nh - 1, 0), 0, zb + gg)),
            pl.BlockSpec((None, tl, wdt), lambda b, i, gg, zb=zb: (b, i, zb + gg)),
            pl.BlockSpec((None, None, halo, wdt),
                         lambda b, i, gg, zb=zb: (b, jnp.minimum((i + 1) * nh, n_hblk - 1), 0, zb + gg)),
            pl.BlockSpec((kw, wdt), lambda b, i, gg, cb=cb: (0, cb + gg)),
            pl.BlockSpec((1, wdt), lambda b, i, gg, cb=cb: (0, cb + gg)),
        ]
        args += [zx4, zx, zx4, conv_w, conv_b.reshape(1, -1)]
    out_specs = [pl.BlockSpec((None, None, tl, wdt), lambda b, i, gg: (b, gg, i, 0)) for wdt in widths]
    out_shape = [jax.ShapeDtypeStruct((bsz, g, seq, wdt), BF16) for wdt in widths]
    scratch = [pltpu.VMEM((tl + 2 * halo, wdt), F32) for wdt in widths]
    return pl.pallas_call(
        functools.partial(_conv_kernel, tl=tl, halo=halo),
        grid=(bsz, seq // tl, g),
        in_specs=in_specs,
        out_specs=out_specs,
        out_shape=out_shape,
        scratch_shapes=scratch,
        compiler_params=_params(("arbitrary", "arbitrary", "arbitrary")),
        name="conv_silu",
    )(*args)


def _ssd_kernel(xs_ref, bm_ref, cm_ref, dt_ref, dtb_ref, alog_ref, y_ref,
                state_ref, cum_ref, cumt_ref, tott_ref, dtc_ref, totc_ref, *, reverse, n_heads, head_dim):
    q = xs_ref.shape[1]
    g_cnt = xs_ref.shape[0]
    r_cnt = n_heads // g_cnt
    off = n_heads if reverse else 0

    @pl.when(pl.program_id(1) == 0)
    def _():
        state_ref[...] = jnp.zeros_like(state_ref)

    dt_raw = dt_ref[...][:, off:off + n_heads]
    dt = jax.nn.softplus(dt_raw + dtb_ref[...][:, off:off + n_heads])
    a = dt * (-jnp.exp(alog_ref[...][:, off:off + n_heads]))
    ri = lax.broadcasted_iota(jnp.int32, (q, q), 0)
    ci = lax.broadcasted_iota(jnp.int32, (q, q), 1)
    keep = (ci >= ri) if reverse else (ci <= ri)
    tri = keep.astype(F32)
    cum = jnp.dot(tri, a, preferred_element_type=F32, precision=HIGHEST)
    ones = jnp.ones((q, q), F32)
    tot = jnp.dot(ones, a, preferred_element_type=F32, precision=HIGHEST)
    cum_t = cum.T
    tot_t = tot.T
    for g in range(g_cnt):
        sl = slice(g * r_cnt, (g + 1) * r_cnt)
        cum_ref[g, :, 0:r_cnt] = cum[:, sl]
        dtc_ref[g, :, 0:r_cnt] = dt[:, sl]
        totc_ref[g, :, 0:r_cnt] = tot[:, sl]
        cumt_ref[g, 0:r_cnt, :] = cum_t[sl, :]
        tott_ref[g, 0:r_cnt, :] = tot_t[sl, :]

    pair = (2 * head_dim == LANES)
    lane = lax.broadcasted_iota(jnp.int32, (q, LANES), 1)
    lo = lane < head_dim

    def expand(cols):
        parts = []
        for r in range(0, r_cnt, 2):
            parts.append(jnp.where(lo, cols[r], cols[r + 1]))
        return jnp.concatenate(parts, axis=1) if len(parts) > 1 else parts[0]

    def group_body(g, carry):
        xg = xs_ref[g].astype(F32)
        bg = bm_ref[g]
        cg = cm_ref[g]
        cum_g = cum_ref[g]
        dt_g = dtc_ref[g]
        cumt_g = cumt_ref[g]
        tott_g = tott_ref[g]
        scores = lax.dot_general(cg, bg, (((1,), (1,)), ((), ())), preferred_element_type=F32)
        cum_cols = [cum_g[:, r:r + 1] for r in range(r_cnt)]
        dt_cols = [dt_g[:, r:r + 1] for r in range(r_cnt)]
        tot_rows = [tott_g[r:r + 1, :] for r in range(r_cnt)]
        dtx = xg * expand(dt_cols)
        tot_g = totc_ref[g]
        wexit = [jnp.exp(tot_g[:, r:r + 1] - cum_cols[r]) for r in range(r_cnt)]
        xw = (dtx * expand(wexit)).astype(BF16)
        dtx_b = dtx.astype(BF16)
        ydiag_parts = []
        for p in range(0, r_cnt, 2):
            xp = dtx_b[:, p * head_dim:(p + 2) * head_dim]
            ys = []
            for r in (p, p + 1):
                seg = cum_cols[r] - cumt_g[r:r + 1, :]
                dec = jnp.where(keep, jnp.exp(seg), 0.0)
                m = (scores * dec).astype(BF16)
                ys.append(jnp.dot(m, xp, preferred_element_type=F32))
            ydiag_parts.append(jnp.where(lo, ys[0], ys[1]))
        y_diag = jnp.concatenate(ydiag_parts, axis=1) if len(ydiag_parts) > 1 else ydiag_parts[0]
        st = state_ref[g]
        y_off = jnp.dot(cg, st.astype(BF16), preferred_element_type=F32)
        ecum = expand([jnp.exp(c) for c in cum_cols])
        y_ref[g] = (y_diag + y_off * ecum).astype(y_ref.dtype)
        s_new = lax.dot_general(bg, xw, (((0,), (0,)), ((), ())), preferred_element_type=F32)
        cdec = expand([jnp.exp(t) for t in tot_rows])
        state_ref[g] = st * cdec[0:st.shape[0], :] + s_new
        return carry

    lax.fori_loop(0, g_cnt, group_body, 0)


def _ssd_sweep(xs, bm, cm, dt_raw, dt_bias, a_log, *, reverse):
    bsz, g, seq, rp = xs.shape
    n = bm.shape[-1]
    h2 = dt_raw.shape[-1]
    n_heads = h2 // 2
    head_dim = rp * g // n_heads
    q = SSD_CHUNK
    nc = seq // q
    assert 2 * head_dim == LANES and n == q and (n_heads // g) % 2 == 0
    cidx = (lambda c: nc - 1 - c) if reverse else (lambda c: c)
    return pl.pallas_call(
        functools.partial(_ssd_kernel, reverse=reverse, n_heads=n_heads, head_dim=head_dim),
        grid=(bsz, nc),
        in_specs=[
            pl.BlockSpec((None, g, q, rp), lambda b, c: (b, 0, cidx(c), 0)),
            pl.BlockSpec((None, g, q, n), lambda b, c: (b, 0, cidx(c), 0)),
            pl.BlockSpec((None, g, q, n), lambda b, c: (b, 0, cidx(c), 0)),
            pl.BlockSpec((None, q, h2), lambda b, c: (b, cidx(c), 0)),
            pl.BlockSpec((1, h2), lambda b, c: (0, 0)),
            pl.BlockSpec((1, h2), lambda b, c: (0, 0)),
        ],
        out_specs=pl.BlockSpec((None, g, q, rp), lambda b, c: (b, 0, cidx(c), 0)),
        out_shape=jax.ShapeDtypeStruct((bsz, g, seq, rp), BF16),
        scratch_shapes=[
            pltpu.VMEM((g, n, rp), F32),
            pltpu.VMEM((g, q, LANES), F32),
            pltpu.VMEM((g, SUBLANES, q), F32),
            pltpu.VMEM((g, SUBLANES, q), F32),
            pltpu.VMEM((g, q, LANES), F32),
            pltpu.VMEM((g, q, LANES), F32),
        ],
        compiler_params=_params(("arbitrary", "arbitrary")),
        name="ssd_bwd" if reverse else "ssd_fwd",
    )(xs, bm, cm, dt_raw, dt_bias.reshape(1, h2), a_log.reshape(1, h2))


def _ssm_out_kernel(yf_ref, yb_ref, xs_ref, z_ref, dsk_ref, ng_ref, w_ref, res_ref, gate_ref, o_ref, h_ref):
    @pl.when(pl.program_id(2) == 0)
    def _():
        g_cnt, _, rp = yf_ref.shape
        for g in range(g_cnt):
            cs = slice(g * rp, (g + 1) * rp)
            y = (yf_ref[g].astype(F32) + yb_ref[g].astype(F32)
                 + dsk_ref[:, cs] * xs_ref[g].astype(F32))
            yz = y * _silu(z_ref[:, cs])
            ms = jnp.mean(yz * yz, axis=-1, keepdims=True)
            h_ref[:, cs] = (yz * lax.rsqrt(ms + GATED_NORM_EPS) * ng_ref[:, cs]).astype(BF16)

    acc = jnp.dot(h_ref[...], w_ref[...], preferred_element_type=F32)
    o_ref[...] = res_ref[...] + gate_ref[...] * acc


def _ssm_out(yf, yb, xs, zx, d_row, norm_g, w_out, x, gate):
    bsz, g, seq, rp = yf.shape
    d_inner = g * rp
    n = w_out.shape[1]
    tm = _tile(seq, 256, SUBLANES * 2)
    tn = _tile(n, 1024, LANES)
    return pl.pallas_call(
        _ssm_out_kernel,
        grid=(bsz, seq // tm, n // tn),
        in_specs=[
            pl.BlockSpec((None, g, tm, rp), lambda b, i, j: (b, 0, i, 0)),
            pl.BlockSpec((None, g, tm, rp), lambda b, i, j: (b, 0, i, 0)),
            pl.BlockSpec((None, g, tm, rp), lambda b, i, j: (b, 0, i, 0)),
            pl.BlockSpec((None, tm, d_inner), lambda b, i, j: (b, i, 0)),
            pl.BlockSpec((1, d_inner), lambda b, i, j: (0, 0)),
            pl.BlockSpec((1, d_inner), lambda b, i, j: (0, 0)),
            pl.BlockSpec((d_inner, tn), lambda b, i, j: (0, j)),
            pl.BlockSpec((None, tm, tn), lambda b, i, j: (b, i, j)),
            pl.BlockSpec((None, 1, tn), lambda b, i, j: (b, 0, j)),
        ],
        out_specs=pl.BlockSpec((None, tm, tn), lambda b, i, j: (b, i, j)),
        out_shape=jax.ShapeDtypeStruct((bsz, seq, n), F32),
        scratch_shapes=[pltpu.VMEM((tm, d_inner), BF16)],
        compiler_params=_params(("arbitrary", "arbitrary", "arbitrary")),
        name="ssm_out",
    )(yf, yb, xs, zx, d_row, norm_g.reshape(1, d_inner), w_out, x, gate.reshape(bsz, 1, n))


def _gmlp_mix_kernel(u_ref, v_ref, lng_ref, lnb_ref, ws_ref, bst_ref, o_ref, *, chunk):
    rows, dg = v_ref.shape
    groups = ws_ref.shape[0]
    gd = dg // groups
    for c in range(rows // chunk):
        rs = slice(c * chunk, (c + 1) * chunk)
        v = v_ref[rs, :].astype(F32)
        mu = jnp.mean(v, axis=-1, keepdims=True)
        vc = v - mu
        var = jnp.mean(vc * vc, axis=-1, keepdims=True)
        vn = (vc * lax.rsqrt(var + NORM_EPS) * lng_ref[...] + lnb_ref[...]).astype(BF16)
        for g in range(groups):
            cs = slice(g * gd, (g + 1) * gd)
            vm = jnp.dot(ws_ref[g], vn[:, cs], preferred_element_type=F32) + bst_ref[:, g:g + 1]
            o_ref[rs, cs] = (u_ref[rs, cs].astype(F32) * vm).astype(o_ref.dtype)


def _gmlp_mix(uv, ln_g, ln_b, w_s, b_s):
    bsz, seq, two_dg = uv.shape
    dg = two_dg // 2
    groups = w_s.shape[0]
    tl = _tile(seq, 256, GMLP_CHUNK)
    return pl.pallas_call(
        functools.partial(_gmlp_mix_kernel, chunk=GMLP_CHUNK),
        grid=(bsz, seq // tl),
        in_specs=[
            pl.BlockSpec((None, tl, dg), lambda b, i: (b, i, 0)),
            pl.BlockSpec((None, tl, dg), lambda b, i: (b, i, 1)),
            pl.BlockSpec((1, dg), lambda b, i: (0, 0)),
            pl.BlockSpec((1, dg), lambda b, i: (0, 0)),
            pl.BlockSpec((groups, GMLP_CHUNK, GMLP_CHUNK), lambda b, i: (0, 0, 0)),
            pl.BlockSpec((GMLP_CHUNK, groups), lambda b, i: (0, 0)),
        ],
        out_specs=pl.BlockSpec((None, tl, dg), lambda b, i: (b, i, 0)),
        out_shape=jax.ShapeDtypeStruct((bsz, seq, dg), BF16),
        compiler_params=_params(("arbitrary", "arbitrary")),
        name="gmlp_mix",
    )(uv, uv, ln_g.reshape(1, dg), ln_b.reshape(1, dg), w_s.astype(BF16), b_s.T)


def _router_kernel(x_ref, g_ref, sh_ref, sc_ref, wrt_ref, br_ref, h_ref, idx_ref, gate_ref, rank_ref,
                   cnt_ref, carry_ref):
    step = pl.program_id(0)

    @pl.when(step == 0)
    def _():
        carry_ref[...] = jnp.zeros_like(carry_ref)

    h = _rms_mod(x_ref[...], g_ref[...], sh_ref[...], sc_ref[...])
    h_ref[...] = _pack_bf16_pairs(h)
    n_exp = wrt_ref.shape[0]
    tm = h.shape[0]
    logits = lax.dot_general(wrt_ref[...], h, (((1,), (1,)), ((), ())),
                             preferred_element_type=F32, precision=HIGHEST) + br_ref[...]
    eidx = lax.broadcasted_iota(jnp.int32, (n_exp, tm), 0).astype(F32)
    work = logits
    tops, idxs, onehots = [], [], []
    for _ in range(TOP_K):
        m = jnp.max(work, axis=0, keepdims=True)
        sel = jnp.min(jnp.where(work == m, eidx, float(n_exp)), axis=0, keepdims=True)
        oh = eidx == sel
        tops.append(m)
        idxs.append(sel.astype(jnp.int32))
        onehots.append(oh)
        work = jnp.where(oh, -jnp.inf, work)
    exps = [jnp.exp(t - tops[0]) for t in tops]
    denom = exps[0] + exps[1] + exps[2] + exps[3]
    member = jnp.zeros((n_exp, tm), F32)
    for oh in onehots:
        member = member + oh.astype(F32)
    ti = lax.broadcasted_iota(jnp.int32, (tm, tm), 0)
    tj = lax.broadcasted_iota(jnp.int32, (tm, tm), 1)
    upper = (ti <= tj).astype(BF16)
    cinc = jnp.dot(member.astype(BF16), upper, preferred_element_type=F32)
    carry = carry_ref[...][:, 0:1]
    cexcl = cinc - member + carry
    ranks = [jnp.sum(jnp.where(oh, cexcl, 0.0), axis=0, keepdims=True) for oh in onehots]
    idx_ref[...] = jnp.concatenate(idxs, axis=0)
    gate_ref[...] = jnp.concatenate([e / denom for e in exps], axis=0)
    rank_ref[...] = jnp.concatenate(ranks, axis=0).astype(jnp.int32)
    new_carry = carry + jnp.sum(member, axis=1, keepdims=True)
    carry_ref[...] = jnp.broadcast_to(new_carry, carry_ref.shape)
    cnt_ref[...] = jnp.broadcast_to(new_carry, cnt_ref.shape).astype(jnp.int32)


def _router(x, norm_g, sh, sc, w_router, b_router):
    bsz, seq, d = x.shape
    t = bsz * seq
    n_exp = w_router.shape[1]
    tm = _tile(seq, 512, LANES)
    per_b = seq // tm
    return pl.pallas_call(
        _router_kernel,
        grid=(t // tm,),
        in_specs=[
            pl.BlockSpec((tm, d), lambda i: (i, 0)),
            pl.BlockSpec((1, d), lambda i: (0, 0)),
            pl.BlockSpec((None, 1, d), lambda i: (i // per_b, 0, 0)),
            pl.BlockSpec((None, 1, d), lambda i: (i // per_b, 0, 0)),
            pl.BlockSpec((n_exp, d), lambda i: (0, 0)),
            pl.BlockSpec((n_exp, 1), lambda i: (0, 0)),
        ],
        out_specs=[
            pl.BlockSpec((tm, d // 2), lambda i: (i, 0)),
            pl.BlockSpec((TOP_K, tm), lambda i: (0, i)),
            pl.BlockSpec((TOP_K, tm), lambda i: (0, i)),
            pl.BlockSpec((TOP_K, tm), lambda i: (0, i)),
            pl.BlockSpec((n_exp, LANES), lambda i: (0, 0)),
        ],
        out_shape=[
            jax.ShapeDtypeStruct((t, d // 2), jnp.int32),
            jax.ShapeDtypeStruct((TOP_K, t), jnp.int32),
            jax.ShapeDtypeStruct((TOP_K, t), F32),
            jax.ShapeDtypeStruct((TOP_K, t), jnp.int32),
            jax.ShapeDtypeStruct((n_exp, LANES), jnp.int32),
        ],
        scratch_shapes=[pltpu.VMEM((n_exp, LANES), F32)],
        compiler_params=_params(("arbitrary",)),
        name="moe_router",
    )(x.reshape(t, d), norm_g.reshape(1, d), sh.reshape(bsz, 1, d), sc.reshape(bsz, 1, d),
      w_router.T, b_router.reshape(n_exp, 1))


def _row_copy(h_hbm, xbuf, sem, tok, slot, r):
    return pltpu.make_async_copy(h_hbm.at[pl.ds(tok, 1)], xbuf.at[slot, pl.ds(r, 1)], sem.at[slot])


def _moe_kernel(ie_ref, nv_ref, cur_ref, nxt_ref, h_hbm, wu_ref, bu_ref, wd_ref, bd_ref, o_ref,
                xbuf, xb16, acc_ref, wub_ref, wdf_ref, wdb_ref, sem):
    it = pl.program_id(0)
    f = pl.program_id(1)
    n_items = pl.num_programs(0)
    n_f = pl.num_programs(1)
    rows, half = xbuf.shape[1], xbuf.shape[2]
    d = 2 * half
    tw = wd_ref.shape[0]
    slot = lax.rem(it, 2)
    valid = nv_ref[it]

    def issue(tok_ref, slot_):
        def body(r, c):
            _row_copy(h_hbm, xbuf, sem, tok_ref[0, r], slot_, r).start()
            return c
        lax.fori_loop(0, rows, body, 0, unroll=8)

    def wait_all(tok_ref, slot_):
        def body(r, c):
            _row_copy(h_hbm, xbuf, sem, tok_ref[0, r], slot_, r).wait()
            return c
        lax.fori_loop(0, rows, body, 0, unroll=8)

    @pl.when(f == 0)
    def _():
        @pl.when((it == 0) & (valid > 0))
        def _():
            issue(cur_ref, slot)

        @pl.when(valid > 0)
        def _():
            wait_all(cur_ref, slot)

        nxt_valid = nv_ref[jnp.minimum(it + 1, n_items - 1)]

        @pl.when((it + 1 < n_items) & (nxt_valid > 0))
        def _():
            issue(nxt_ref, 1 - slot)

        @pl.when(valid > 0)
        def _():
            lo, hi = _unpack_bf16_pairs(xbuf[slot])
            xb16[:, 0:half] = lo.astype(BF16)
            xb16[:, half:d] = hi.astype(BF16)

    @pl.when((valid > 0) & (f == 0))
    def _():
        acc_ref[...] = jnp.broadcast_to(bd_ref[...], acc_ref.shape)

    def compute(m):
        wub_ref[...] = wu_ref[...].astype(BF16)
        hw = tw // 2
        for s in range(d // LANES):
            cs = slice(s * LANES, (s + 1) * LANES)
            wdf_ref[s, pl.ds(0, hw, stride=2), :] = wd_ref[0:hw, cs]
            wdf_ref[s, pl.ds(1, hw, stride=2), :] = wd_ref[hw:tw, cs]
        for s in range(d // LANES):
            wdb_ref[:, s * LANES:(s + 1) * LANES] = wdf_ref[s].astype(BF16)
        x = xb16[0:m, :]
        b = bu_ref[...]
        gu1 = jnp.dot(x, wub_ref[:, 0:tw], preferred_element_type=F32) + b[:, 0:tw]
        gu2 = jnp.dot(x, wub_ref[:, tw:2 * tw], preferred_element_type=F32) + b[:, tw:2 * tw]
        lane = lax.broadcasted_iota(jnp.int32, (m, tw), 1)
        even = (lane & 1) == 0
        gate = jnp.where(even, gu1, pltpu.roll(gu2, 1, 1))
        lin = jnp.where(even, pltpu.roll(gu1, tw - 1, 1), gu2)
        gate = jnp.minimum(gate, SWIGLU_LIMIT)
        lin = jnp.clip(lin, -SWIGLU_LIMIT, SWIGLU_LIMIT)
        act = gate * jax.nn.sigmoid(SWIGLU_ALPHA * gate) * (lin + 1.0)
        part = jnp.dot(act.astype(BF16), wdb_ref[...], preferred_element_type=F32)

        acc_ref[0:m, :] += part

        @pl.when(f == n_f - 1)
        def _():
            o_ref[0:m, :] = _pack_bf16_pairs(acc_ref[0:m, :])
            if m < rows:
                o_ref[m:rows, :] = jnp.zeros((rows - m, half), jnp.int32)

    step = rows // MOE_ROW_BUCKETS
    for m in range(step, rows + 1, step):
        pl.when((valid > m - step) & (valid <= m))(functools.partial(compute, m))

    @pl.when((valid == 0) & (f == n_f - 1))
    def _():
        o_ref[...] = jnp.zeros(o_ref.shape, jnp.int32)


def _moe_experts(h, row_token, item_expert, item_rows, w_up, b_up, w_down, b_down):
    t, half = h.shape
    d = 2 * half
    n_exp, _, two_f = w_up.shape
    n_items = item_expert.shape[0]
    rows = MOE_ROWS
    tw = min(MOE_TW, two_f // 2)
    n_f = two_f // (2 * tw)
    rt = row_token.reshape(n_items, 1, rows)

    def fsel(i, f, nv):
        return jnp.where(nv[i] > 0, f, n_f - 1)

    grid_spec = pltpu.PrefetchScalarGridSpec(
        num_scalar_prefetch=2,
        grid=(n_items, n_f),
        in_specs=[
            pl.BlockSpec((None, 1, rows), lambda i, f, ie, nv: (i, 0, 0), memory_space=pltpu.SMEM),
            pl.BlockSpec((None, 1, rows), lambda i, f, ie, nv: (jnp.minimum(i + 1, n_items - 1), 0, 0),
                         memory_space=pltpu.SMEM),
            pl.BlockSpec(memory_space=pl.ANY),
            pl.BlockSpec((None, d, 2 * tw), lambda i, f, ie, nv: (ie[i], 0, fsel(i, f, nv))),
            pl.BlockSpec((None, 1, 2 * tw), lambda i, f, ie, nv: (ie[i], 0, fsel(i, f, nv))),
            pl.BlockSpec((None, tw, d), lambda i, f, ie, nv: (ie[i], fsel(i, f, nv), 0)),
            pl.BlockSpec((None, 1, d), lambda i, f, ie, nv: (ie[i], 0, 0)),
        ],
        out_specs=pl.BlockSpec((rows, half), lambda i, f, ie, nv: (i, 0)),
        scratch_shapes=[
            pltpu.VMEM((2, rows, half), jnp.int32),
            pltpu.VMEM((rows, d), BF16),
            pltpu.VMEM((rows, d), F32),
            pltpu.VMEM((d, 2 * tw), BF16),
            pltpu.VMEM((d // LANES, tw, LANES), F32),
            pltpu.VMEM((tw, d), BF16),
            pltpu.SemaphoreType.DMA((2,)),
        ],
    )
    return pl.pallas_call(
        _moe_kernel,
        grid_spec=grid_spec,
        out_shape=jax.ShapeDtypeStruct((n_items * rows, half), jnp.int32),
        compiler_params=_params(("arbitrary", "arbitrary")),
        name="moe_experts",
    )(item_expert, item_rows, rt, rt, h, w_up, b_up.reshape(n_exp, 1, two_f), w_down,
      b_down.reshape(n_exp, 1, d))


def _combine_copy(y_hbm, buf, sem, row, slot, r):
    return pltpu.make_async_copy(y_hbm.at[pl.ds(row, 1)], buf.at[slot, pl.ds(r, 1)], sem.at[slot])


def _combine_kernel(cur_ref, nxt_ref, y_hbm, x_ref, gate_ref, g2_ref, fg_ref, o_ref, buf, sem, *, final_norm):
    it = pl.program_id(0)
    n_tiles = pl.num_programs(0)
    nrow = buf.shape[1]
    tq = x_ref.shape[0]
    slot = lax.rem(it, 2)

    def issue(ref, slot_):
        def body(r, c):
            _combine_copy(y_hbm, buf, sem, ref[0, r], slot_, r).start()
            return c
        lax.fori_loop(0, nrow, body, 0, unroll=8)

    def wait_all(ref, slot_):
        def body(r, c):
            _combine_copy(y_hbm, buf, sem, ref[0, r], slot_, r).wait()
            return c
        lax.fori_loop(0, nrow, body, 0, unroll=8)

    @pl.when(it == 0)
    def _():
        issue(cur_ref, slot)

    wait_all(cur_ref, slot)

    @pl.when(it + 1 < n_tiles)
    def _():
        issue(nxt_ref, 1 - slot)

    gates = gate_ref[...]
    half = buf.shape[2]
    d = 2 * half
    y_lo = jnp.zeros((tq, half), F32)
    y_hi = jnp.zeros((tq, half), F32)
    for k in range(TOP_K):
        lo, hi = _unpack_bf16_pairs(buf[slot, k * tq:(k + 1) * tq, :])
        gk = gates[:, k:k + 1]
        y_lo = y_lo + gk * lo
        y_hi = y_hi + gk * hi
    out_lo = x_ref[:, 0:half] + g2_ref[:, 0:half] * y_lo
    out_hi = x_ref[:, half:d] + g2_ref[:, half:d] * y_hi
    if final_norm:
        ssq = (jnp.sum(out_lo * out_lo, axis=-1, keepdims=True)
               + jnp.sum(out_hi * out_hi, axis=-1, keepdims=True))
        inv = lax.rsqrt(ssq * (1.0 / d) + NORM_EPS)
        out_lo = out_lo * inv * fg_ref[:, 0:half]
        out_hi = out_hi * inv * fg_ref[:, half:d]
    o_ref[:, 0:half] = out_lo
    o_ref[:, half:d] = out_hi


def _moe_combine(y_rows, dest_km, gates_tk, x, g2, final_g, *, final_norm):
    bsz, seq, d = x.shape
    t = bsz * seq
    tq = _tile(seq, COMBINE_TOKENS, SUBLANES)
    n_tiles = t // tq
    per_b = seq // tq
    dd = dest_km.reshape(n_tiles, 1, TOP_K * tq)
    grid_spec = pltpu.PrefetchScalarGridSpec(
        num_scalar_prefetch=0,
        grid=(n_tiles,),
        in_specs=[
            pl.BlockSpec((None, 1, TOP_K * tq), lambda i: (i, 0, 0), memory_space=pltpu.SMEM),
            pl.BlockSpec((None, 1, TOP_K * tq), lambda i: (jnp.minimum(i + 1, n_tiles - 1), 0, 0),
                         memory_space=pltpu.SMEM),
            pl.BlockSpec(memory_space=pl.ANY),
            pl.BlockSpec((tq, d), lambda i: (i, 0)),
            pl.BlockSpec((tq, TOP_K), lambda i: (i, 0)),
            pl.BlockSpec((None, 1, d), lambda i: (i // per_b, 0, 0)),
            pl.BlockSpec((1, d), lambda i: (0, 0)),
        ],
        out_specs=pl.BlockSpec((tq, d), lambda i: (i, 0)),
        scratch_shapes=[
            pltpu.VMEM((2, TOP_K * tq, d // 2), jnp.int32),
            pltpu.SemaphoreType.DMA((2,)),
        ],
    )
    out = pl.pallas_call(
        functools.partial(_combine_kernel, final_norm=final_norm),
        grid_spec=grid_spec,
        out_shape=jax.ShapeDtypeStruct((t, d), F32),
        compiler_params=_params(("arbitrary",)),
        name="moe_combine",
    )(dd, dd, y_rows, x.reshape(t, d), gates_tk, g2.reshape(bsz, 1, d), final_g.reshape(1, d))
    return out.reshape(bsz, seq, d)


def _moe_ffn(x, norm_g, sh, sc, g2, w_router, b_router, w_up, b_up, w_down, b_down, final_g, *, final_norm):
    bsz, seq, d = x.shape
    t = bsz * seq
    n_exp = w_router.shape[1]
    rows = MOE_ROWS
    h, idx, gates, rank, cnt = _router(x, norm_g, sh, sc, w_router, b_router)
    counts = cnt[:, 0]
    per_exp = (counts + rows - 1) // rows
    it_end = jnp.cumsum(per_exp)
    it_start = it_end - per_exp
    starts = it_start * rows
    experts = jnp.arange(n_exp, dtype=jnp.int32)
    dest = jnp.sum(jnp.where(idx[..., None] == experts, starts, 0), axis=-1) + rank
    n_items = (t * TOP_K) // rows + n_exp
    n_rows = n_items * rows
    tok = jnp.broadcast_to(jnp.arange(t, dtype=jnp.int32)[None, :], (TOP_K, t))
    row_token = jnp.zeros((n_rows,), jnp.int32).at[dest.reshape(-1)].set(tok.reshape(-1))
    items = jnp.arange(n_items, dtype=jnp.int32)
    used = it_end[-1]
    item_pos = jnp.minimum(items, used - 1)
    item_expert = jnp.sum(item_pos[:, None] >= it_end[None, :], axis=1).astype(jnp.int32)
    onehot_e = item_expert[:, None] == experts[None, :]
    first_item = jnp.sum(jnp.where(onehot_e, it_start[None, :], 0), axis=1)
    exp_count = jnp.sum(jnp.where(onehot_e, counts[None, :], 0), axis=1)
    item_rows = jnp.where(items < used, jnp.clip(exp_count - (items - first_item) * rows, 0, rows),
                          0).astype(jnp.int32)
    y_rows = _moe_experts(h, row_token, item_expert, item_rows, w_up, b_up, w_down, b_down)
    tq = _tile(seq, COMBINE_TOKENS, SUBLANES)
    dest_km = dest.reshape(TOP_K, t // tq, tq).transpose(1, 0, 2).reshape(-1)
    return _moe_combine(y_rows, dest_km, gates.T, x, g2, final_g, final_norm=final_norm)


def _mamba_mixer(x, norm_g, sh, sc, g1, w_in, conv_w, conv_b, dt_bias, a_log, d_skip, ssm_norm_g, w_out):
    d_inner = ssm_norm_g.shape[0]
    n_heads = d_skip.shape[0]
    head_dim = d_inner // n_heads
    d_xbc = conv_w.shape[1]
    n_state = (d_xbc - d_inner) // (2 * SSM_GROUPS)
    n_main = d_inner + d_xbc
    norm = (norm_g, sh, sc)
    zx = _mm(x, w_in[:, :n_main].astype(BF16), prologue="norm", norm=norm, name="ssm_in")
    dt_raw = _mm(x, w_in[:, n_main:].astype(BF16), prologue="norm", norm=norm, name="ssm_in_dt")
    xs, bm, cm = _conv_silu(zx, conv_w, conv_b, d_inner, n_state)
    yf = _ssd_sweep(xs, bm, cm, dt_raw, dt_bias, a_log, reverse=False)
    yb = _ssd_sweep(xs, bm, cm, dt_raw, dt_bias, a_log, reverse=True)
    d_row = jnp.repeat(d_skip.astype(F32), head_dim).reshape(1, d_inner)
    return _ssm_out(yf, yb, xs, zx, d_row, ssm_norm_g, w_out.astype(BF16), x, g1)


def _gmlp_mixer(x, norm_g, sh, sc, g1, w_in, b_in, ln_g, ln_b, w_s, b_s, w_out, b_out):
    uv = _mm(x, w_in.astype(BF16), prologue="norm", norm=(norm_g, sh, sc), epilogue="bias_gelu",
             bias=b_in, out_dtype=BF16, name="gmlp_in")
    gated = _gmlp_mix(uv, ln_g, ln_b, w_s, b_s)
    return _mm(gated, w_out.astype(BF16), epilogue="resid", bias=b_out, resid=x, gate=g1, name="gmlp_out")


def kernel(x, c, ada_w, ada_b, norm_mix_g, norm_ffn_g, ssm_w_in, ssm_conv_w, ssm_conv_b, ssm_dt_bias,
           ssm_a_log, ssm_d, ssm_norm_g, ssm_w_out, gmlp_w_in, gmlp_b_in, gmlp_ln_g, gmlp_ln_b, gmlp_w_s,
           gmlp_b_s, gmlp_w_out, gmlp_b_out, moe_w_router, moe_b_router, moe_w_up, moe_b_up, moe_w_down,
           moe_b_down, final_g):
    depth = ada_w.shape[0]
    d = x.shape[-1]
    mod = _ada_mod(c, ada_w, ada_b)
    for i in range(depth):
        sh1, sc1, g1, sh2, sc2, g2 = (mod[i, :, k * d:(k + 1) * d] for k in range(6))
        j = i // 2
        if i % 2 == 0:
            x = _mamba_mixer(x, norm_mix_g[i], sh1, sc1, g1, ssm_w_in[j], ssm_conv_w[j], ssm_conv_b[j],
                             ssm_dt_bias[j], ssm_a_log[j], ssm_d[j], ssm_norm_g[j], ssm_w_out[j])
        else:
            x = _gmlp_mixer(x, norm_mix_g[i], sh1, sc1, g1, gmlp_w_in[j], gmlp_b_in[j], gmlp_ln_g[j],
                            gmlp_ln_b[j], gmlp_w_s[j], gmlp_b_s[j], gmlp_w_out[j], gmlp_b_out[j])
        x = _moe_ffn(x, norm_ffn_g[i], sh2, sc2, g2, moe_w_router[i], moe_b_router[i], moe_w_up[i],
                     moe_b_up[i], moe_w_down[i], moe_b_down[i], final_g, final_norm=(i == depth - 1))
    return x
```

```python
import functools

import jax
import jax.numpy as jnp
from jax import lax
from jax.experimental import pallas as pl
from jax.experimental.pallas import tpu as pltpu

F32 = jnp.float32
BF16 = jnp.bfloat16
HIGHEST = lax.Precision.HIGHEST

NORM_EPS = 1e-6
GATED_NORM_EPS = 1e-5
SSM_GROUPS = 8
SSD_CHUNK = 128
GMLP_CHUNK = 128
TOP_K = 4
SWIGLU_ALPHA = 1.702
SWIGLU_LIMIT = 7.0

VMEM_LIMIT_BYTES = 56 * 1024 * 1024
LANES = 128
SUBLANES = 8
MOE_ROWS = 1024
MOE_ROW_BUCKETS = 4
MOE_TW = 256
COMBINE_TOKENS = 256


def _tile(n, pref, mult):
    if n <= pref:
        return n
    t = (pref // mult) * mult
    while t >= mult:
        if n % t == 0:
            return t
        t -= mult
    return n


def _params(sem):
    return pltpu.CompilerParams(dimension_semantics=sem, vmem_limit_bytes=VMEM_LIMIT_BYTES)


def _silu(x):
    return x * jax.nn.sigmoid(x)


def _gelu_erf(x):
    return 0.5 * x * (1.0 + lax.erf(x * (2.0 ** -0.5)))


def _pack_bf16_pairs(x):
    half = x.shape[1] // 2
    xb = x.astype(BF16).astype(F32)
    lo = lax.shift_right_logical(lax.bitcast_convert_type(xb[:, :half], jnp.int32), 16)
    hi = lax.bitcast_convert_type(xb[:, half:], jnp.int32) & jnp.int32(-65536)
    return lo | hi


def _unpack_bf16_pairs(p):
    lo = lax.bitcast_convert_type(lax.shift_left(p, 16), F32)
    hi = lax.bitcast_convert_type(p & jnp.int32(-65536), F32)
    return lo, hi


def _rms_mod(x, g, sh, sc):
    ms = jnp.mean(x * x, axis=-1, keepdims=True)
    y = x * lax.rsqrt(ms + NORM_EPS) * g
    return y * (1.0 + sc) + sh


def _ada_kernel(c_ref, w_ref, b_ref, o_ref):
    ca = _silu(c_ref[...])
    o_ref[...] = jnp.dot(ca, w_ref[...], preferred_element_type=F32, precision=HIGHEST) + b_ref[...]


def _ada_mod(c, ada_w, ada_b):
    depth, d, n = ada_w.shape
    bsz = c.shape[0]
    tn = _tile(n, 1024, LANES)
    return pl.pallas_call(
        _ada_kernel,
        grid=(depth, n // tn),
        in_specs=[
            pl.BlockSpec((bsz, d), lambda l, j: (0, 0)),
            pl.BlockSpec((None, d, tn), lambda l, j: (l, 0, j)),
            pl.BlockSpec((None, 1, tn), lambda l, j: (l, 0, j)),
        ],
        out_specs=pl.BlockSpec((None, bsz, tn), lambda l, j: (l, 0, j)),
        out_shape=jax.ShapeDtypeStruct((depth, bsz, n), F32),
        compiler_params=_params(("arbitrary", "arbitrary")),
        name="ada_mod",
    )(c, ada_w, ada_b.reshape(depth, 1, n))


def _mm_kernel(*refs, prologue, epilogue):
    it = iter(refs)
    a_ref = next(it)
    if prologue == "norm":
        g_ref, sh_ref, sc_ref = next(it), next(it), next(it)
    w_ref = next(it)
    if epilogue in ("bias_gelu", "resid"):
        bias_ref = next(it)
    if epilogue == "resid":
        res_ref, gate_ref = next(it), next(it)
    o_ref = next(it)
    if prologue == "norm":
        h_ref = next(it)

        @pl.when(pl.program_id(2) == 0)
        def _():
            h = _rms_mod(a_ref[...], g_ref[...], sh_ref[...], sc_ref[...])
            h_ref[...] = h.astype(BF16)

        a = h_ref[...]
    else:
        a = a_ref[...]
    acc = jnp.dot(a, w_ref[...], preferred_element_type=F32)
    if epilogue == "store":
        o_ref[...] = acc.astype(o_ref.dtype)
    elif epilogue == "bias_gelu":
        o_ref[...] = _gelu_erf(acc + bias_ref[...]).astype(o_ref.dtype)
    elif epilogue == "resid":
        o_ref[...] = res_ref[...] + gate_ref[...] * (acc + bias_ref[...])


def _mm(a, w, *, prologue=None, norm=None, epilogue="store", bias=None, resid=None, gate=None,
        out_dtype=F32, tm_pref=1024, tn_pref=1024, name="mm"):
    bsz, seq, k = a.shape
    n = w.shape[1]
    tm = _tile(seq, tm_pref, SUBLANES * 2)
    tn = _tile(n, tn_pref, LANES)
    grid = (bsz, seq // tm, n // tn)
    in_specs = [pl.BlockSpec((None, tm, k), lambda b, i, j: (b, i, 0))]
    args = [a]
    scratch = []
    if prologue == "norm":
        g, sh, sc = norm
        in_specs += [
            pl.BlockSpec((1, k), lambda b, i, j: (0, 0)),
            pl.BlockSpec((None, 1, k), lambda b, i, j: (b, 0, 0)),
            pl.BlockSpec((None, 1, k), lambda b, i, j: (b, 0, 0)),
        ]
        args += [g.reshape(1, k), sh.reshape(bsz, 1, k), sc.reshape(bsz, 1, k)]
        scratch = [pltpu.VMEM((tm, k), BF16)]
    in_specs.append(pl.BlockSpec((k, tn), lambda b, i, j: (0, j)))
    args.append(w)
    if epilogue in ("bias_gelu", "resid"):
        in_specs.append(pl.BlockSpec((1, tn), lambda b, i, j: (0, j)))
        args.append(bias.reshape(1, n).astype(F32))
    if epilogue == "resid":
        in_specs += [
            pl.BlockSpec((None, tm, tn), lambda b, i, j: (b, i, j)),
            pl.BlockSpec((None, 1, tn), lambda b, i, j: (b, 0, j)),
        ]
        args += [resid, gate.reshape(bsz, 1, n)]
    return pl.pallas_call(
        functools.partial(_mm_kernel, prologue=prologue, epilogue=epilogue),
        grid=grid,
        in_specs=in_specs,
        out_specs=pl.BlockSpec((None, tm, tn), lambda b, i, j: (b, i, j)),
        out_shape=jax.ShapeDtypeStruct((bsz, seq, n), out_dtype),
        scratch_shapes=scratch,
        compiler_params=_params(("arbitrary", "arbitrary", "arbitrary")),
        name=name,
    )(*args)


def _conv_kernel(*refs, tl, halo):
    ins, outs, scr = refs[:15], refs[15:18], refs[18:21]
    i = pl.program_id(1)
    first = i == 0
    last = i == pl.num_programs(1) - 1
    for s in range(3):
        prev_ref, main_ref, next_ref, w_ref, b_ref = ins[5 * s:5 * s + 5]
        buf = scr[s]
        o_ref = outs[s]
        zeros = jnp.zeros((halo, buf.shape[1]), F32)
        buf[0:halo, :] = jnp.where(first, zeros, prev_ref[...])
        buf[halo:halo + tl, :] = main_ref[...]
        buf[halo + tl:2 * halo + tl, :] = jnp.where(last, zeros, next_ref[...])
        w = w_ref[...]
        kw = w.shape[0]
        acc = jnp.zeros((tl, buf.shape[1]), F32) + b_ref[...]
        for k in range(kw):
            acc = acc + buf[pl.ds(halo - kw // 2 + k, tl), :] * w[k:k + 1, :]
        o_ref[...] = _silu(acc).astype(o_ref.dtype)


def _conv_silu(zx, conv_w, conv_b, d_inner, n_state):
    bsz, seq, _ = zx.shape
    g = SSM_GROUPS
    rp = d_inner // g
    kw = conv_w.shape[0]
    halo = SUBLANES
    tl = _tile(seq, 512, SUBLANES)
    nh = tl // halo
    n_hblk = seq // halo
    widths = (rp, n_state, n_state)
    zoffs = (d_inner, 2 * d_inner, 2 * d_inner + g * n_state)
    coffs = (0, d_inner, d_inner + g * n_state)
    zx4 = zx.reshape(bsz, n_hblk, halo, zx.shape[-1])
    in_specs, args = [], []
    for wdt, zo, co in zip(widths, zoffs, coffs):
        zb, cb = zo // wdt, co // wdt
        in_specs += [
            pl.BlockSpec((None, None, halo, wdt),
                         lambda b, i, gg, zb=zb: (b, jnp.maximum(i * nh - 1, 0), 0, zb + gg)),
            pl.BlockSpec((None, tl, wdt), lambda b, i, gg, zb=zb: (b, i, zb + gg)),
            pl.BlockSpec((None, None, halo, wdt),
                         lambda b, i, gg, zb=zb: (b, jnp.minimum((i + 1) * nh, n_hblk - 1), 0, zb + gg)),
            pl.BlockSpec((kw, wdt), lambda b, i, gg, cb=cb: (0, cb + gg)),
            pl.BlockSpec((1, wdt), lambda b, i, gg, cb=cb: (0, cb + gg)),
        ]
        args += [zx4, zx, zx4, conv_w, conv_b.reshape(1, -1)]
    out_specs = [pl.BlockSpec((None, None, tl, wdt), lambda b, i, gg: (b, gg, i, 0)) for wdt in widths]
    out_shape = [jax.ShapeDtypeStruct((bsz, g, seq, wdt), BF16) for wdt in widths]
    scratch = [pltpu.VMEM((tl + 2 * halo, wdt), F32) for wdt in widths]
    return pl.pallas_call(
        functools.partial(_conv_kernel, tl=tl, halo=halo),
        grid=(bsz, seq // tl, g),
        in_specs=in_specs,
        out_specs=out_specs,
        out_shape=out_shape,
        scratch_shapes=scratch,
        compiler_params=_params(("arbitrary", "arbitrary", "arbitrary")),
        name="conv_silu",
    )(*args)


def _ssd_kernel(xs_ref, bm_ref, cm_ref, dt_ref, dtb_ref, alog_ref, y_ref,
                state_ref, cum_ref, cumt_ref, tott_ref, dtc_ref, *, reverse, n_heads, head_dim):
    q = xs_ref.shape[1]
    g_cnt = xs_ref.shape[0]
    r_cnt = n_heads // g_cnt
    off = n_heads if reverse else 0

    @pl.when(pl.program_id(1) == 0)
    def _():
        state_ref[...] = jnp.zeros_like(state_ref)

    dt_raw = dt_ref[...][:, off:off + n_heads]
    dt = jax.nn.softplus(dt_raw + dtb_ref[...][:, off:off + n_heads])
    a = dt * (-jnp.exp(alog_ref[...][:, off:off + n_heads]))
    ri = lax.broadcasted_iota(jnp.int32, (q, q), 0)
    ci = lax.broadcasted_iota(jnp.int32, (q, q), 1)
    keep = (ci >= ri) if reverse else (ci <= ri)
    tri = keep.astype(F32)
    cum = jnp.dot(tri, a, preferred_element_type=F32, precision=HIGHEST)
    ones = jnp.ones((q, q), F32)
    tot = jnp.dot(ones, a, preferred_element_type=F32, precision=HIGHEST)
    cum_t = cum.T
    tot_t = tot.T
    for g in range(g_cnt):
        sl = slice(g * r_cnt, (g + 1) * r_cnt)
        cum_ref[g, :, 0:r_cnt] = cum[:, sl]
        dtc_ref[g, :, 0:r_cnt] = dt[:, sl]
        cumt_ref[g, 0:r_cnt, :] = cum_t[sl, :]
        tott_ref[g, 0:r_cnt, :] = tot_t[sl, :]

    lane = lax.broadcasted_iota(jnp.int32, (q, LANES), 1)
    lo = lane < head_dim

    def expand(cols):
        parts = []
        for r in range(0, r_cnt, 2):
            parts.append(jnp.where(lo, cols[r], cols[r + 1]))
        return jnp.concatenate(parts, axis=1) if len(parts) > 1 else parts[0]

    def group_body(g, carry):
        xg = xs_ref[g].astype(F32)
        bg = bm_ref[g]
        cg = cm_ref[g]
        cum_g = cum_ref[g]
        dt_g = dtc_ref[g]
        cumt_g = cumt_ref[g]
        tott_g = tott_ref[g]
        scores = lax.dot_general(cg, bg, (((1,), (1,)), ((), ())), preferred_element_type=F32)
        cum_cols = [jnp.broadcast_to(cum_g[:, r:r + 1], (q, LANES)) for r in range(r_cnt)]
        dt_cols = [jnp.broadcast_to(dt_g[:, r:r + 1], (q, LANES)) for r in range(r_cnt)]
        tot_rows = [jnp.broadcast_to(tott_g[r:r + 1, :], (q, LANES)) for r in range(r_cnt)]
        dtx = xg * expand(dt_cols)
        wexit = [jnp.exp(tot_rows[r] - cum_cols[r]) for r in range(r_cnt)]
        xw = (dtx * expand(wexit)).astype(BF16)
        dtx_b = dtx.astype(BF16)
        ydiag_parts = []
        for p in range(0, r_cnt, 2):
            xp = dtx_b[:, p * head_dim:(p + 2) * head_dim]
            ys = []
            for r in (p, p + 1):
                seg = cum_cols[r] - cumt_g[r:r + 1, :]
                dec = jnp.where(keep, jnp.exp(seg), 0.0)
                m = (scores * dec).astype(BF16)
                ys.append(jnp.dot(m, xp, preferred_element_type=F32))
            ydiag_parts.append(jnp.where(lo, ys[0], ys[1]))
        y_diag = jnp.concatenate(ydiag_parts, axis=1) if len(ydiag_parts) > 1 else ydiag_parts[0]
        st = state_ref[g]
        y_off = jnp.dot(cg, st.astype(BF16), preferred_element_type=F32)
        ecum = expand([jnp.exp(c) for c in cum_cols])
        y_ref[g] = (y_diag + y_off * ecum).astype(y_ref.dtype)
        s_new = lax.dot_general(bg, xw, (((0,), (0,)), ((), ())), preferred_element_type=F32)
        cdec = expand([jnp.exp(t) for t in tot_rows])
        state_ref[g] = st * cdec[0:st.shape[0], :] + s_new
        return carry

    lax.fori_loop(0, g_cnt, group_body, 0)


def _ssd_sweep(xs, bm, cm, dt_raw, dt_bias, a_log, *, reverse):
    bsz, g, seq, rp = xs.shape
    n = bm.shape[-1]
    h2 = dt_raw.shape[-1]
    n_heads = h2 // 2
    head_dim = rp * g // n_heads
    q = SSD_CHUNK
    nc = seq // q
    assert 2 * head_dim == LANES and n == q and (n_heads // g) % 2 == 0
    cidx = (lambda c: nc - 1 - c) if reverse else (lambda c: c)
    return pl.pallas_call(
        functools.partial(_ssd_kernel, reverse=reverse, n_heads=n_heads, head_dim=head_dim),
        grid=(bsz, nc),
        in_specs=[
            pl.BlockSpec((None, g, q, rp), lambda b, c: (b, 0, cidx(c), 0)),
            pl.BlockSpec((None, g, q, n), lambda b, c: (b, 0, cidx(c), 0)),
            pl.BlockSpec((None, g, q, n), lambda b, c: (b, 0, cidx(c), 0)),
            pl.BlockSpec((None, q, h2), lambda b, c: (b, cidx(c), 0)),
            pl.BlockSpec((1, h2), lambda b, c: (0, 0)),
            pl.BlockSpec((1, h2), lambda b, c: (0, 0)),
        ],
        out_specs=pl.BlockSpec((None, g, q, rp), lambda b, c: (b, 0, cidx(c), 0)),
        out_shape=jax.ShapeDtypeStruct((bsz, g, seq, rp), BF16),
        scratch_shapes=[
            pltpu.VMEM((g, n, rp), F32),
            pltpu.VMEM((g, q, LANES), F32),
            pltpu.VMEM((g, SUBLANES, q), F32),
            pltpu.VMEM((g, SUBLANES, q), F32),
            pltpu.VMEM((g, q, LANES), F32),
        ],
        compiler_params=_params(("arbitrary", "arbitrary")),
        name="ssd_bwd" if reverse else "ssd_fwd",
    )(xs, bm, cm, dt_raw, dt_bias.reshape(1, h2), a_log.reshape(1, h2))


def _ssm_out_kernel(yf_ref, yb_ref, xs_ref, z_ref, dsk_ref, ng_ref, w_ref, res_ref, gate_ref, o_ref):
    g_cnt, _, rp = yf_ref.shape
    acc = None
    for g in range(g_cnt):
        cs = slice(g * rp, (g + 1) * rp)
        y = (yf_ref[g].astype(F32) + yb_ref[g].astype(F32)
             + dsk_ref[:, cs] * xs_ref[g].astype(F32))
        yz = y * _silu(z_ref[:, cs])
        ms = jnp.mean(yz * yz, axis=-1, keepdims=True)
        h = (yz * lax.rsqrt(ms + GATED_NORM_EPS) * ng_ref[:, cs]).astype(BF16)
        part = jnp.dot(h, w_ref[cs, :], preferred_element_type=F32)
        acc = part if acc is None else acc + part
    o_ref[...] = res_ref[...] + gate_ref[...] * acc


def _ssm_out(yf, yb, xs, zx, d_row, norm_g, w_out, x, gate):
    bsz, g, seq, rp = yf.shape
    d_inner = g * rp
    n = w_out.shape[1]
    tm = _tile(seq, 256, SUBLANES * 2)
    return pl.pallas_call(
        _ssm_out_kernel,
        grid=(bsz, seq // tm),
        in_specs=[
            pl.BlockSpec((None, g, tm, rp), lambda b, i: (b, 0, i, 0)),
            pl.BlockSpec((None, g, tm, rp), lambda b, i: (b, 0, i, 0)),
            pl.BlockSpec((None, g, tm, rp), lambda b, i: (b, 0, i, 0)),
            pl.BlockSpec((None, tm, d_inner), lambda b, i: (b, i, 0)),
            pl.BlockSpec((1, d_inner), lambda b, i: (0, 0)),
            pl.BlockSpec((1, d_inner), lambda b, i: (0, 0)),
            pl.BlockSpec((d_inner, n), lambda b, i: (0, 0), pipeline_mode=pl.Buffered(1)),
            pl.BlockSpec((None, tm, n), lambda b, i: (b, i, 0)),
            pl.BlockSpec((None, 1, n), lambda b, i: (b, 0, 0)),
        ],
        out_specs=pl.BlockSpec((None, tm, n), lambda b, i: (b, i, 0)),
        out_shape=jax.ShapeDtypeStruct((bsz, seq, n), F32),
        compiler_params=_params(("arbitrary", "arbitrary")),
        name="ssm_out",
    )(yf, yb, xs, zx, d_row, norm_g.reshape(1, d_inner), w_out, x, gate.reshape(bsz, 1, n))


def _gmlp_mix_kernel(u_ref, v_ref, lng_ref, lnb_ref, ws_ref, bst_ref, o_ref, *, chunk):
    rows, dg = v_ref.shape
    groups = ws_ref.shape[0]
    gd = dg // groups
    for c in range(rows // chunk):
        rs = slice(c * chunk, (c + 1) * chunk)
        v = v_ref[rs, :].astype(F32)
        mu = jnp.mean(v, axis=-1, keepdims=True)
        vc = v - mu
        var = jnp.mean(vc * vc, axis=-1, keepdims=True)
        vn = (vc * lax.rsqrt(var + NORM_EPS) * lng_ref[...] + lnb_ref[...]).astype(BF16)
        for g in range(groups):
            cs = slice(g * gd, (g + 1) * gd)
            vm = jnp.dot(ws_ref[g], vn[:, cs], preferred_element_type=F32) + bst_ref[:, g:g + 1]
            o_ref[rs, cs] = (u_ref[rs, cs].astype(F32) * vm).astype(o_ref.dtype)


def _gmlp_mix(uv, ln_g, ln_b, w_s, b_s):
    bsz, seq, two_dg = uv.shape
    dg = two_dg // 2
    groups = w_s.shape[0]
    tl = _tile(seq, 256, GMLP_CHUNK)
    return pl.pallas_call(
        functools.partial(_gmlp_mix_kernel, chunk=GMLP_CHUNK),
        grid=(bsz, seq // tl),
        in_specs=[
            pl.BlockSpec((None, tl, dg), lambda b, i: (b, i, 0)),
            pl.BlockSpec((None, tl, dg), lambda b, i: (b, i, 1)),
            pl.BlockSpec((1, dg), lambda b, i: (0, 0)),
            pl.BlockSpec((1, dg), lambda b, i: (0, 0)),
            pl.BlockSpec((groups, GMLP_CHUNK, GMLP_CHUNK), lambda b, i: (0, 0, 0)),
            pl.BlockSpec((GMLP_CHUNK, groups), lambda b, i: (0, 0)),
        ],
        out_specs=pl.BlockSpec((None, tl, dg), lambda b, i: (b, i, 0)),
        out_shape=jax.ShapeDtypeStruct((bsz, seq, dg), BF16),
        compiler_params=_params(("arbitrary", "arbitrary")),
        name="gmlp_mix",
    )(uv, uv, ln_g.reshape(1, dg), ln_b.reshape(1, dg), w_s.astype(BF16), b_s.T)


def _router_kernel(x_ref, g_ref, sh_ref, sc_ref, wrt_ref, br_ref, h_ref, idx_ref, gate_ref, rank_ref,
                   cnt_ref, carry_ref):
    step = pl.program_id(0)

    @pl.when(step == 0)
    def _():
        carry_ref[...] = jnp.zeros_like(carry_ref)

    h = _rms_mod(x_ref[...], g_ref[...], sh_ref[...], sc_ref[...])
    h_ref[...] = _pack_bf16_pairs(h)
    n_exp = wrt_ref.shape[0]
    tm = h.shape[0]
    logits = lax.dot_general(wrt_ref[...], h, (((1,), (1,)), ((), ())),
                             preferred_element_type=F32, precision=HIGHEST) + br_ref[...]
    eidx = lax.broadcasted_iota(jnp.int32, (n_exp, tm), 0).astype(F32)
    work = logits
    tops, idxs, onehots = [], [], []
    for _ in range(TOP_K):
        m = jnp.max(work, axis=0, keepdims=True)
        sel = jnp.min(jnp.where(work == m, eidx, float(n_exp)), axis=0, keepdims=True)
        oh = eidx == sel
        tops.append(m)
        idxs.append(sel.astype(jnp.int32))
        onehots.append(oh)
        work = jnp.where(oh, -jnp.inf, work)
    exps = [jnp.exp(t - tops[0]) for t in tops]
    denom = exps[0] + exps[1] + exps[2] + exps[3]
    member = jnp.zeros((n_exp, tm), F32)
    for oh in onehots:
        member = member + oh.astype(F32)
    ti = lax.broadcasted_iota(jnp.int32, (tm, tm), 0)
    tj = lax.broadcasted_iota(jnp.int32, (tm, tm), 1)
    upper = (ti <= tj).astype(BF16)
    cinc = jnp.dot(member.astype(BF16), upper, preferred_element_type=F32)
    carry = carry_ref[...][:, 0:1]
    cexcl = cinc - member + carry
    ranks = [jnp.sum(jnp.where(oh, cexcl, 0.0), axis=0, keepdims=True) for oh in onehots]
    idx_ref[...] = jnp.concatenate(idxs, axis=0)
    gate_ref[...] = jnp.concatenate([e / denom for e in exps], axis=0)
    rank_ref[...] = jnp.concatenate(ranks, axis=0).astype(jnp.int32)
    new_carry = carry + jnp.sum(member, axis=1, keepdims=True)
    carry_ref[...] = jnp.broadcast_to(new_carry, carry_ref.shape)
    cnt_ref[...] = jnp.broadcast_to(new_carry, cnt_ref.shape).astype(jnp.int32)


def _router(x, norm_g, sh, sc, w_router, b_router):
    bsz, seq, d = x.shape
    t = bsz * seq
    n_exp = w_router.shape[1]
    tm = _tile(seq, 512, LANES)
    per_b = seq // tm
    return pl.pallas_call(
        _router_kernel,
        grid=(t // tm,),
        in_specs=[
            pl.BlockSpec((tm, d), lambda i: (i, 0)),
            pl.BlockSpec((1, d), lambda i: (0, 0)),
            pl.BlockSpec((None, 1, d), lambda i: (i // per_b, 0, 0)),
            pl.BlockSpec((None, 1, d), lambda i: (i // per_b, 0, 0)),
            pl.BlockSpec((n_exp, d), lambda i: (0, 0)),
            pl.BlockSpec((n_exp, 1), lambda i: (0, 0)),
        ],
        out_specs=[
            pl.BlockSpec((tm, d // 2), lambda i: (i, 0)),
            pl.BlockSpec((TOP_K, tm), lambda i: (0, i)),
            pl.BlockSpec((TOP_K, tm), lambda i: (0, i)),
            pl.BlockSpec((TOP_K, tm), lambda i: (0, i)),
            pl.BlockSpec((n_exp, LANES), lambda i: (0, 0)),
        ],
        out_shape=[
            jax.ShapeDtypeStruct((t, d // 2), jnp.int32),
            jax.ShapeDtypeStruct((TOP_K, t), jnp.int32),
            jax.ShapeDtypeStruct((TOP_K, t), F32),
            jax.ShapeDtypeStruct((TOP_K, t), jnp.int32),
            jax.ShapeDtypeStruct((n_exp, LANES), jnp.int32),
        ],
        scratch_shapes=[pltpu.VMEM((n_exp, LANES), F32)],
        compiler_params=_params(("arbitrary",)),
        name="moe_router",
    )(x.reshape(t, d), norm_g.reshape(1, d), sh.reshape(bsz, 1, d), sc.reshape(bsz, 1, d),
      w_router.T, b_router.reshape(n_exp, 1))


ROW_DMA_PRIORITY = 1


def _row_copy(h_hbm, xbuf, sem, tok, slot, r):
    return pltpu.make_async_copy(h_hbm.at[pl.ds(tok, 1)], xbuf.at[slot, pl.ds(r, 1)], sem.at[slot])


def _moe_kernel(ie_ref, nv_ref, cur_ref, nxt_ref, h_hbm, wu_ref, bu_ref, wd_ref, bd_ref, o_ref,
                xbuf, xb16, acc_ref, wub_ref, wdf_ref, wdb_ref, sem):
    it = pl.program_id(0)
    f = pl.program_id(1)
    n_items = pl.num_programs(0)
    n_f = pl.num_programs(1)
    rows, half = xbuf.shape[1], xbuf.shape[2]
    d = 2 * half
    tw = wd_ref.shape[0]
    slot = lax.rem(it, 2)
    valid = nv_ref[it]

    def issue(tok_ref, slot_):
        def body(r, c):
            _row_copy(h_hbm, xbuf, sem, tok_ref[0, r], slot_, r).start(priority=ROW_DMA_PRIORITY)
            return c
        lax.fori_loop(0, rows, body, 0, unroll=8)

    def wait_all(tok_ref, slot_):
        def body(r, c):
            _row_copy(h_hbm, xbuf, sem, tok_ref[0, r], slot_, r).wait()
            return c
        lax.fori_loop(0, rows, body, 0, unroll=8)

    @pl.when(f == 0)
    def _():
        @pl.when((it == 0) & (valid > 0))
        def _():
            issue(cur_ref, slot)

        @pl.when(valid > 0)
        def _():
            wait_all(cur_ref, slot)

        nxt_valid = nv_ref[jnp.minimum(it + 1, n_items - 1)]

        @pl.when((it + 1 < n_items) & (nxt_valid > 0))
        def _():
            issue(nxt_ref, 1 - slot)

        @pl.when(valid > 0)
        def _():
            lo, hi = _unpack_bf16_pairs(xbuf[slot])
            xb16[:, 0:half] = lo.astype(BF16)
            xb16[:, half:d] = hi.astype(BF16)

    @pl.when((valid > 0) & (f == 0))
    def _():
        acc_ref[...] = jnp.broadcast_to(bd_ref[...], acc_ref.shape)

    def compute(m):
        wub_ref[...] = wu_ref[...].astype(BF16)
        hw = tw // 2
        for s in range(d // LANES):
            cs = slice(s * LANES, (s + 1) * LANES)
            wdf_ref[s, pl.ds(0, hw, stride=2), :] = wd_ref[0:hw, cs]
            wdf_ref[s, pl.ds(1, hw, stride=2), :] = wd_ref[hw:tw, cs]
        for s in range(d // LANES):
            wdb_ref[:, s * LANES:(s + 1) * LANES] = wdf_ref[s].astype(BF16)
        x = xb16[0:m, :]
        b = bu_ref[...]
        gu1 = jnp.dot(x, wub_ref[:, 0:tw], preferred_element_type=F32) + b[:, 0:tw]
        gu2 = jnp.dot(x, wub_ref[:, tw:2 * tw], preferred_element_type=F32) + b[:, tw:2 * tw]
        lane = lax.broadcasted_iota(jnp.int32, (m, tw), 1)
        even = (lane & 1) == 0
        gate = jnp.where(even, gu1, pltpu.roll(gu2, 1, 1))
        lin = jnp.where(even, pltpu.roll(gu1, tw - 1, 1), gu2)
        gate = jnp.minimum(gate, SWIGLU_LIMIT)
        lin = jnp.clip(lin, -SWIGLU_LIMIT, SWIGLU_LIMIT)
        act = gate * jax.nn.sigmoid(SWIGLU_ALPHA * gate) * (lin + 1.0)
        part = jnp.dot(act.astype(BF16), wdb_ref[...], preferred_element_type=F32)

        acc_ref[0:m, :] += part

        @pl.when(f == n_f - 1)
        def _():
            o_ref[0:m, :] = _pack_bf16_pairs(acc_ref[0:m, :])
            if m < rows:
                o_ref[m:rows, :] = jnp.zeros((rows - m, half), jnp.int32)

    step = rows // MOE_ROW_BUCKETS
    for m in range(step, rows + 1, step):
        pl.when((valid > m - step) & (valid <= m))(functools.partial(compute, m))

    @pl.when((valid == 0) & (f == n_f - 1))
    def _():
        o_ref[...] = jnp.zeros(o_ref.shape, jnp.int32)


def _moe_experts(h, row_token, item_expert, item_rows, w_up, b_up, w_down, b_down, layer):
    t, half = h.shape
    d = 2 * half
    depth, n_exp, _, two_f = w_up.shape
    n_items = item_expert.shape[0]
    rows = MOE_ROWS
    tw = min(MOE_TW, two_f // 2)
    n_f = two_f // (2 * tw)
    rt = row_token.reshape(n_items, 1, rows)

    def fsel(i, f, nv):
        return jnp.where(nv[i] > 0, f, n_f - 1)

    grid_spec = pltpu.PrefetchScalarGridSpec(
        num_scalar_prefetch=2,
        grid=(n_items, n_f),
        in_specs=[
            pl.BlockSpec((None, 1, rows), lambda i, f, ie, nv: (i, 0, 0), memory_space=pltpu.SMEM),
            pl.BlockSpec((None, 1, rows), lambda i, f, ie, nv: (jnp.minimum(i + 1, n_items - 1), 0, 0),
                         memory_space=pltpu.SMEM),
            pl.BlockSpec(memory_space=pl.ANY),
            pl.BlockSpec((None, None, d, 2 * tw), lambda i, f, ie, nv: (layer, ie[i], 0, fsel(i, f, nv))),
            pl.BlockSpec((None, None, 1, 2 * tw), lambda i, f, ie, nv: (layer, ie[i], 0, fsel(i, f, nv))),
            pl.BlockSpec((None, None, tw, d), lambda i, f, ie, nv: (layer, ie[i], fsel(i, f, nv), 0)),
            pl.BlockSpec((None, None, 1, d), lambda i, f, ie, nv: (layer, ie[i], 0, 0)),
        ],
        out_specs=pl.BlockSpec((rows, half), lambda i, f, ie, nv: (i, 0)),
        scratch_shapes=[
            pltpu.VMEM((2, rows, half), jnp.int32),
            pltpu.VMEM((rows, d), BF16),
            pltpu.VMEM((rows, d), F32),
            pltpu.VMEM((d, 2 * tw), BF16),
            pltpu.VMEM((d // LANES, tw, LANES), F32),
            pltpu.VMEM((tw, d), BF16),
            pltpu.SemaphoreType.DMA((2,)),
        ],
    )
    return pl.pallas_call(
        _moe_kernel,
        grid_spec=grid_spec,
        out_shape=jax.ShapeDtypeStruct((n_items * rows, half), jnp.int32),
        compiler_params=_params(("arbitrary", "arbitrary")),
        name="moe_experts",
    )(item_expert, item_rows, rt, rt, h, w_up, b_up.reshape(depth, n_exp, 1, two_f), w_down,
      b_down.reshape(depth, n_exp, 1, d))


def _combine_copy(y_hbm, buf, sem, row, slot, r):
    return pltpu.make_async_copy(y_hbm.at[pl.ds(row, 1)], buf.at[slot, pl.ds(r, 1)], sem.at[slot])


def _combine_kernel(cur_ref, nxt_ref, y_hbm, x_ref, gate_ref, g2_ref, fg_ref, o_ref, buf, sem, *, final_norm):
    it = pl.program_id(0)
    n_tiles = pl.num_programs(0)
    nrow = buf.shape[1]
    tq = x_ref.shape[0]
    slot = lax.rem(it, 2)

    def issue(ref, slot_):
        def body(i, c):
            for p in range(2):
                r = 2 * i + p
                _combine_copy(y_hbm, buf, sem, ref[0, r], slot_, r).start(priority=p)
            return c
        lax.fori_loop(0, nrow // 2, body, 0, unroll=4)

    def wait_all(ref, slot_):
        def body(r, c):
            _combine_copy(y_hbm, buf, sem, ref[0, r], slot_, r).wait()
            return c
        lax.fori_loop(0, nrow, body, 0, unroll=8)

    @pl.when(it == 0)
    def _():
        issue(cur_ref, slot)

    wait_all(cur_ref, slot)

    @pl.when(it + 1 < n_tiles)
    def _():
        issue(nxt_ref, 1 - slot)

    gates = gate_ref[...]
    half = buf.shape[2]
    d = 2 * half
    y_lo = jnp.zeros((tq, half), F32)
    y_hi = jnp.zeros((tq, half), F32)
    for k in range(TOP_K):
        lo, hi = _unpack_bf16_pairs(buf[slot, k * tq:(k + 1) * tq, :])
        gk = gates[:, k:k + 1]
        y_lo = y_lo + gk * lo
        y_hi = y_hi + gk * hi
    out_lo = x_ref[:, 0:half] + g2_ref[:, 0:half] * y_lo
    out_hi = x_ref[:, half:d] + g2_ref[:, half:d] * y_hi
    if final_norm:
        ssq = (jnp.sum(out_lo * out_lo, axis=-1, keepdims=True)
               + jnp.sum(out_hi * out_hi, axis=-1, keepdims=True))
        inv = lax.rsqrt(ssq * (1.0 / d) + NORM_EPS)
        out_lo = out_lo * inv * fg_ref[:, 0:half]
        out_hi = out_hi * inv * fg_ref[:, half:d]
    o_ref[:, 0:half] = out_lo
    o_ref[:, half:d] = out_hi


def _moe_combine(y_rows, dest_km, gates_tk, x, g2, final_g, *, final_norm):
    bsz, seq, d = x.shape
    t = bsz * seq
    tq = _tile(seq, COMBINE_TOKENS, SUBLANES)
    n_tiles = t // tq
    per_b = seq // tq
    dd = dest_km.reshape(n_tiles, 1, TOP_K * tq)
    grid_spec = pltpu.PrefetchScalarGridSpec(
        num_scalar_prefetch=0,
        grid=(n_tiles,),
        in_specs=[
            pl.BlockSpec((None, 1, TOP_K * tq), lambda i: (i, 0, 0), memory_space=pltpu.SMEM),
            pl.BlockSpec((None, 1, TOP_K * tq), lambda i: (jnp.minimum(i + 1, n_tiles - 1), 0, 0),
                         memory_space=pltpu.SMEM),
            pl.BlockSpec(memory_space=pl.ANY),
            pl.BlockSpec((tq, d), lambda i: (i, 0)),
            pl.BlockSpec((tq, TOP_K), lambda i: (i, 0)),
            pl.BlockSpec((None, 1, d), lambda i: (i // per_b, 0, 0)),
            pl.BlockSpec((1, d), lambda i: (0, 0)),
        ],
        out_specs=pl.BlockSpec((tq, d), lambda i: (i, 0)),
        scratch_shapes=[
            pltpu.VMEM((2, TOP_K * tq, d // 2), jnp.int32),
            pltpu.SemaphoreType.DMA((2,)),
        ],
    )
    out = pl.pallas_call(
        functools.partial(_combine_kernel, final_norm=final_norm),
        grid_spec=grid_spec,
        out_shape=jax.ShapeDtypeStruct((t, d), F32),
        compiler_params=_params(("arbitrary",)),
        name="moe_combine",
    )(dd, dd, y_rows, x.reshape(t, d), gates_tk, g2.reshape(bsz, 1, d), final_g.reshape(1, d))
    return out.reshape(bsz, seq, d)


def _moe_ffn(x, norm_g, sh, sc, g2, w_router, b_router, w_up, b_up, w_down, b_down, final_g, *, layer,
             final_norm):
    bsz, seq, d = x.shape
    t = bsz * seq
    n_exp = w_router.shape[1]
    rows = MOE_ROWS
    h, idx, gates, rank, cnt = _router(x, norm_g, sh, sc, w_router, b_router)
    counts = cnt[:, 0]
    per_exp = (counts + rows - 1) // rows
    it_end = jnp.cumsum(per_exp)
    it_start = it_end - per_exp
    starts = it_start * rows
    experts = jnp.arange(n_exp, dtype=jnp.int32)
    dest = jnp.sum(jnp.where(idx[..., None] == experts, starts, 0), axis=-1) + rank
    n_items = (t * TOP_K) // rows + n_exp
    n_rows = n_items * rows
    tok = jnp.broadcast_to(jnp.arange(t, dtype=jnp.int32)[None, :], (TOP_K, t))
    row_token = jnp.zeros((n_rows,), jnp.int32).at[dest.reshape(-1)].set(tok.reshape(-1))
    items = jnp.arange(n_items, dtype=jnp.int32)
    used = it_end[-1]
    item_pos = jnp.minimum(items, used - 1)
    item_expert = jnp.sum(item_pos[:, None] >= it_end[None, :], axis=1).astype(jnp.int32)
    onehot_e = item_expert[:, None] == experts[None, :]
    first_item = jnp.sum(jnp.where(onehot_e, it_start[None, :], 0), axis=1)
    exp_count = jnp.sum(jnp.where(onehot_e, counts[None, :], 0), axis=1)
    item_rows = jnp.where(items < used, jnp.clip(exp_count - (items - first_item) * rows, 0, rows),
                          0).astype(jnp.int32)
    y_rows = _moe_experts(h, row_token, item_expert, item_rows, w_up, b_up, w_down, b_down, layer)
    tq = _tile(seq, COMBINE_TOKENS, SUBLANES)
    dest_km = dest.reshape(TOP_K, t // tq, tq).transpose(1, 0, 2).reshape(-1)
    return _moe_combine(y_rows, dest_km, gates.T, x, g2, final_g, final_norm=final_norm)


def _mamba_mixer(x, norm_g, sh, sc, g1, w_in, conv_w, conv_b, dt_bias, a_log, d_skip, ssm_norm_g, w_out):
    d_inner = ssm_norm_g.shape[0]
    n_heads = d_skip.shape[0]
    head_dim = d_inner // n_heads
    d_xbc = conv_w.shape[1]
    n_state = (d_xbc - d_inner) // (2 * SSM_GROUPS)
    n_main = d_inner + d_xbc
    norm = (norm_g, sh, sc)
    zx = _mm(x, w_in[:, :n_main].astype(BF16), prologue="norm", norm=norm, name="ssm_in")
    dt_raw = _mm(x, w_in[:, n_main:].astype(BF16), prologue="norm", norm=norm, name="ssm_in_dt")
    xs, bm, cm = _conv_silu(zx, conv_w, conv_b, d_inner, n_state)
    yf = _ssd_sweep(xs, bm, cm, dt_raw, dt_bias, a_log, reverse=False)
    yb = _ssd_sweep(xs, bm, cm, dt_raw, dt_bias, a_log, reverse=True)
    d_row = jnp.repeat(d_skip.astype(F32), head_dim).reshape(1, d_inner)
    return _ssm_out(yf, yb, xs, zx, d_row, ssm_norm_g, w_out.astype(BF16), x, g1)


def _gmlp_mixer(x, norm_g, sh, sc, g1, w_in, b_in, ln_g, ln_b, w_s, b_s, w_out, b_out):
    uv = _mm(x, w_in.astype(BF16), prologue="norm", norm=(norm_g, sh, sc), epilogue="bias_gelu",
             bias=b_in, out_dtype=BF16, name="gmlp_in")
    gated = _gmlp_mix(uv, ln_g, ln_b, w_s, b_s)
    return _mm(gated, w_out.astype(BF16), epilogue="resid", bias=b_out, resid=x, gate=g1, name="gmlp_out")


def kernel(x, c, ada_w, ada_b, norm_mix_g, norm_ffn_g, ssm_w_in, ssm_conv_w, ssm_conv_b, ssm_dt_bias,
           ssm_a_log, ssm_d, ssm_norm_g, ssm_w_out, gmlp_w_in, gmlp_b_in, gmlp_ln_g, gmlp_ln_b, gmlp_w_s,
           gmlp_b_s, gmlp_w_out, gmlp_b_out, moe_w_router, moe_b_router, moe_w_up, moe_b_up, moe_w_down,
           moe_b_down, final_g):
    depth = ada_w.shape[0]
    d = x.shape[-1]
    mod = _ada_mod(c, ada_w, ada_b)
    for i in range(depth):
        sh1, sc1, g1, sh2, sc2, g2 = (mod[i, :, k * d:(k + 1) * d] for k in range(6))
        j = i // 2
        if i % 2 == 0:
            x = _mamba_mixer(x, norm_mix_g[i], sh1, sc1, g1, ssm_w_in[j], ssm_conv_w[j], ssm_conv_b[j],
                             ssm_dt_bias[j], ssm_a_log[j], ssm_d[j], ssm_norm_g[j], ssm_w_out[j])
        else:
            x = _gmlp_mixer(x, norm_mix_g[i], sh1, sc1, g1, gmlp_w_in[j], gmlp_b_in[j], gmlp_ln_g[j],
                            gmlp_ln_b[j], gmlp_w_s[j], gmlp_b_s[j], gmlp_w_out[j], gmlp_b_out[j])
        x = _moe_ffn(x, norm_ffn_g[i], sh2, sc2, g2, moe_w_router[i], moe_b_router[i], moe_w_up, moe_b_up,
                     moe_w_down, moe_b_down, final_g, layer=i, final_norm=(i == depth - 1))
    return x
```
